```python
import math
import jax, jax.numpy as jnp
from jax import lax
import numpy as np

D_MODEL = 2048
BATCH = 16
SEQ = 2048
DEPTH = 1

GRID_W = 64
N_DIFF_HEADS = 8
DIFF_QK_DIM = 64
DIFF_V_DIM = 2 * DIFF_QK_DIM
DIFF_QK_COLS = N_DIFF_HEADS * 2 * DIFF_QK_DIM
DIFF_WIDTH = N_DIFF_HEADS * DIFF_V_DIM
Q_BLOCK = 128
N_NA_HEADS = 8
NA_HEAD_DIM = 128
NA_WIDTH = N_NA_HEADS * NA_HEAD_DIM
NA_KR_MAX = 8
NA_KC = 16
NA_SPAN = 2 * NA_KC
IN_COLS = 2 * DIFF_QK_COLS + DIFF_WIDTH + 3 * NA_WIDTH + 2 * D_MODEL
D_FF = 5632
CONV_W = 3
LN_EPS = 1e-5
RMS_EPS = 1e-5
DEEPNORM_ALPHA = (2.0 * DEPTH) ** 0.25
DEEPNORM_BETA = (8.0 * DEPTH) ** -0.25

kernel_name = "hybrid_diffattn_natten_convffn_deepnorm"


def layer_norm(x, g, b):
    xf = x.astype(jnp.float32)
    mu = jnp.mean(xf, -1, keepdims=True)
    var = jnp.mean(jnp.square(xf - mu), -1, keepdims=True)
    return ((xf - mu) * lax.rsqrt(var + LN_EPS) * g.astype(jnp.float32) + b.astype(jnp.float32)).astype(x.dtype)


def rms_norm(x, g):
    xf = x.astype(jnp.float32)
    ms = jnp.mean(jnp.square(xf), -1, keepdims=True)
    return (xf * lax.rsqrt(ms + RMS_EPS) * g.astype(jnp.float32)).astype(x.dtype)


def alibi_slopes(n):
    return np.array([2.0 ** (-8.0 * (i + 1) / n) for i in range(n)], dtype=np.float32)


def diff_attention(q1, q2, k1, k2, v, lam):
    B, H, S, _ = q1.shape
    nblk = S // Q_BLOCK
    scale = DIFF_QK_DIM ** -0.5
    slopes = jnp.asarray(alibi_slopes(H))[:, None, None]
    kpos = jnp.arange(S, dtype=jnp.int32)

    def to_blocks(q):
        return q.reshape(B, H, nblk, Q_BLOCK, q.shape[-1]).transpose(2, 0, 1, 3, 4)

    def one_block(args):
        q1b, q2b, start = args
        qpos = start + jnp.arange(Q_BLOCK, dtype=jnp.int32)
        bias = -slopes * jnp.abs(qpos[:, None] - kpos[None, :]).astype(jnp.float32)
        s1 = jnp.einsum('bhqd,bhkd->bhqk', q1b, k1).astype(jnp.float32) * scale + bias
        s2 = jnp.einsum('bhqd,bhkd->bhqk', q2b, k2).astype(jnp.float32) * scale + bias
        a = jax.nn.softmax(s1, axis=-1) - lam * jax.nn.softmax(s2, axis=-1)
        return jnp.einsum('bhqk,bhkd->bhqd', a.astype(v.dtype), v)

    starts = jnp.arange(nblk, dtype=jnp.int32) * Q_BLOCK
    out = lax.map(one_block, (to_blocks(q1), to_blocks(q2), starts))
    return out.transpose(1, 2, 0, 3, 4).reshape(B, H, S, v.shape[-1])


def neighbourhood_attention(q, k, v, rpb):
    B, H, S, d = q.shape
    rows = S // GRID_W
    kr = min(NA_KR_MAX, rows)
    nb = GRID_W // NA_KC
    scale = d ** -0.5
    r = np.arange(rows)
    row_start = np.clip(r - kr // 2, 0, rows - kr)
    dr_idx = row_start[:, None] + np.arange(kr)[None, :] - r[:, None] + (NA_KR_MAX - 1)
    c = np.arange(GRID_W)
    col_start = np.clip(c - NA_KC // 2, 0, GRID_W - NA_KC)
    span_start = np.clip(np.arange(nb) * NA_KC - NA_KC // 2, 0, GRID_W - NA_SPAN)
    span_cols = span_start[:, None] + np.arange(NA_SPAN)[None, :]
    key_col = span_cols[c // NA_KC]
    col_valid = (key_col >= col_start[:, None]) & (key_col < col_start[:, None] + NA_KC)
    dc_idx = np.clip(key_col - c[:, None] + (NA_KC - 1), 0, 2 * NA_KC - 2)
    bias = rpb[:, dr_idx[:, None, :, None], dc_idx[None, :, None, :]].astype(jnp.float32)
    bias = jnp.where(jnp.asarray(col_valid)[None, None, :, None, :], bias, -jnp.inf)
    bias_rows = bias.transpose(1, 0, 2, 3, 4).reshape(rows, H, nb, NA_KC, kr, NA_SPAN)

    q_rows = q.reshape(B, H, rows, nb, NA_KC, d).transpose(2, 0, 1, 3, 4, 5)
    k_grid = k.reshape(B, H, rows, GRID_W, d)
    v_grid = v.reshape(B, H, rows, GRID_W, d)

    def one_row(args):
        q_row, bias_row, r0 = args
        k_strip = lax.dynamic_slice_in_dim(k_grid, r0, kr, axis=2)
        v_strip = lax.dynamic_slice_in_dim(v_grid, r0, kr, axis=2)
        k_blk = k_strip[:, :, :, span_cols]
        v_blk = v_strip[:, :, :, span_cols]
        s = jnp.einsum('bhnqd,bhrnkd->bhnqrk', q_row, k_blk).astype(jnp.float32) * scale + bias_row
        p = jax.nn.softmax(s, axis=(-2, -1))
        return jnp.einsum('bhnqrk,bhrnkd->bhnqd', p.astype(v.dtype), v_blk)

    out = lax.map(one_row, (q_rows, bias_rows, jnp.asarray(row_start, dtype=jnp.int32)))
    return out.transpose(1, 2, 0, 3, 4, 5).reshape(B, H, S, d)


def token_mixer(x, w_in, b_in, lam_q1, lam_k1, lam_q2, lam_k2, subln_g, rpb,
                w_branch_a, w_branch_b, w_out, lambda_init):
    B, S, _ = x.shape
    z = jnp.einsum('bsd,dc->bsc', x, w_in) + b_in
    splits = [int(i) for i in np.cumsum([DIFF_QK_COLS, DIFF_QK_COLS, DIFF_WIDTH,
                                         NA_WIDTH, NA_WIDTH, NA_WIDTH, D_MODEL])]
    qa, ka, va, qb, kb, vb, ga, gb = jnp.split(z, splits, axis=-1)

    qa = qa.reshape(B, S, N_DIFF_HEADS, 2, DIFF_QK_DIM).transpose(3, 0, 2, 1, 4)
    ka = ka.reshape(B, S, N_DIFF_HEADS, 2, DIFF_QK_DIM).transpose(3, 0, 2, 1, 4)
    va = va.reshape(B, S, N_DIFF_HEADS, DIFF_V_DIM).transpose(0, 2, 1, 3)
    lam = (jnp.exp(jnp.sum(lam_q1.astype(jnp.float32) * lam_k1.astype(jnp.float32)))
           - jnp.exp(jnp.sum(lam_q2.astype(jnp.float32) * lam_k2.astype(jnp.float32)))
           + lambda_init)
    oa = diff_attention(qa[0], qa[1], ka[0], ka[1], va, lam)
    oa = rms_norm(oa, subln_g) * (1.0 - lambda_init)
    oa = oa.transpose(0, 2, 1, 3).reshape(B, S, DIFF_WIDTH)

    def heads(t):
        return t.reshape(B, S, N_NA_HEADS, NA_HEAD_DIM).transpose(0, 2, 1, 3)
    ob = neighbourhood_attention(heads(qb), heads(kb), heads(vb), rpb)
    ob = ob.transpose(0, 2, 1, 3).reshape(B, S, NA_WIDTH)

    y = (jax.nn.sigmoid(ga) * jnp.einsum('bsc,cd->bsd', oa, w_branch_a)
         + jax.nn.sigmoid(gb) * jnp.einsum('bsc,cd->bsd', ob, w_branch_b))
    return jnp.einsum('bsd,de->bse', y, w_out)


def conv_ffn(x, w_up, conv_w, conv_b, w_down):
    S = x.shape[1]
    h = jnp.einsum('bsd,df->bsf', x, w_up)
    pad = CONV_W // 2
    hp = jnp.pad(h, ((0, 0), (pad, pad), (0, 0)))
    hc = conv_b + hp[:, 0:S] * conv_w[0]
    for i in range(1, CONV_W):
        hc = hc + hp[:, i:i + S] * conv_w[i]
    gate, val = jnp.split(hc, 2, axis=-1)
    return jnp.einsum('bsf,fd->bsd', jax.nn.gelu(gate, approximate=False) * val, w_down)


def setup_inputs(seed: int = 0) -> dict:
    key = jax.random.key(seed)
    ks = jax.random.split(key, 21)
    f32 = jnp.float32
    n = lambda k, shape, s: jax.random.normal(k, shape, f32) * s
    L = DEPTH
    return {
        "x": jax.random.normal(ks[0], (BATCH, SEQ, D_MODEL), f32),
        "w_in": n(ks[1], (L, D_MODEL, IN_COLS), D_MODEL ** -0.5),
        "b_in": n(ks[2], (L, IN_COLS), 0.02),
        "lam_q1": n(ks[3], (L, DIFF_QK_DIM), 0.1),
        "lam_k1": n(ks[4], (L, DIFF_QK_DIM), 0.1),
        "lam_q2": n(ks[5], (L, DIFF_QK_DIM), 0.1),
        "lam_k2": n(ks[6], (L, DIFF_QK_DIM), 0.1),
        "subln_g": 1.0 + n(ks[7], (L, DIFF_V_DIM), 0.02),
        "rpb": n(ks[8], (L, N_NA_HEADS, 2 * NA_KR_MAX - 1, 2 * NA_KC - 1), 0.02),
        "w_branch_a": n(ks[9], (L, DIFF_WIDTH, D_MODEL), DIFF_WIDTH ** -0.5),
        "w_branch_b": n(ks[10], (L, NA_WIDTH, D_MODEL), NA_WIDTH ** -0.5),
        "w_out": n(ks[11], (L, D_MODEL, D_MODEL), DEEPNORM_BETA * D_MODEL ** -0.5),
        "ln1_g": 1.0 + n(ks[12], (L, D_MODEL), 0.02),
        "ln1_b": n(ks[13], (L, D_MODEL), 0.02),
        "w_up": n(ks[14], (L, D_MODEL, 2 * D_FF), D_MODEL ** -0.5),
        "conv_w": n(ks[15], (L, CONV_W, 2 * D_FF), CONV_W ** -0.5),
        "conv_b": n(ks[16], (L, 2 * D_FF), 0.02),
        "w_down": n(ks[17], (L, D_FF, D_MODEL), DEEPNORM_BETA * D_FF ** -0.5),
        "ln2_g": 1.0 + n(ks[18], (L, D_MODEL), 0.02),
        "ln2_b": n(ks[19], (L, D_MODEL), 0.02),
    }


def reference(x, w_in, b_in, lam_q1, lam_k1, lam_q2, lam_k2, subln_g, rpb,
              w_branch_a, w_branch_b, w_out, ln1_g, ln1_b,
              w_up, conv_w, conv_b, w_down, ln2_g, ln2_b):
    for l in range(DEPTH):
        lambda_init = 0.8 - 0.6 * math.exp(-0.3 * l)
        h = token_mixer(x, w_in[l], b_in[l], lam_q1[l], lam_k1[l], lam_q2[l], lam_k2[l],
                        subln_g[l], rpb[l], w_branch_a[l], w_branch_b[l], w_out[l], lambda_init)
        x = layer_norm(DEEPNORM_ALPHA * x + h, ln1_g[l], ln1_b[l])
        f = conv_ffn(x, w_up[l], conv_w[l], conv_b[l], w_down[l])
        x = layer_norm(DEEPNORM_ALPHA * x + f, ln2_g[l], ln2_b[l])
    return x
```

```python
import math
from functools import partial

import jax
import jax.numpy as jnp
import numpy as np
from jax import lax
from jax.experimental import pallas as pl
from jax.experimental.pallas import tpu as pltpu

F32 = jnp.float32
BF16 = jnp.bfloat16

D_MODEL = 2048
SEQ = 2048
GRID_W = 64
GRID_ROWS = SEQ // GRID_W
N_DIFF_HEADS = 8
DIFF_QK_DIM = 64
DIFF_V_DIM = 2 * DIFF_QK_DIM
DIFF_WIDTH = N_DIFF_HEADS * DIFF_V_DIM
N_NA_HEADS = 8
NA_HEAD_DIM = 128
NA_WIDTH = N_NA_HEADS * NA_HEAD_DIM
NA_KR = 8
NA_KC = 16
IN_COLS = 3 * DIFF_WIDTH + 3 * NA_WIDTH + 2 * D_MODEL
D_FF = 5632
LN_EPS = 1e-5
RMS_EPS = 1e-5
DEEPNORM_ALPHA = 2.0 ** 0.25
LAMBDA_INIT = 0.8 - 0.6 * math.exp(-0.3 * 0)
LOG2E = 1.4426950408889634

HEAD_COLS = 128
QA_BLK, KA_BLK, VA_BLK = 0, 8, 16
QB_BLK, KB_BLK, VB_BLK = 24, 32, 40
GATE_COL0 = 3 * DIFF_WIDTH + 3 * NA_WIDTH

MIB = 1024 * 1024
INPROJ_TM, INPROJ_TN = 1024, 1024
DIFF_TQ = 256
NA_QROWS = 4
NA_KROWS = 12
NA_TQ = NA_QROWS * GRID_W
NA_TK = NA_KROWS * GRID_W
NA_GROUPS = GRID_ROWS // NA_QROWS
MERGE_TM = 512
FFN_TN = 512
FFN_NJ = D_FF // FFN_TN
DOWN_TM, DOWN_TK = 1024, 512


def _cparams(n_axes, vmem_mib):
    return pltpu.CompilerParams(
        dimension_semantics=("arbitrary",) * n_axes,
        vmem_limit_bytes=vmem_mib * MIB,
    )


def _inproj_kernel(x_ref, w_ref, b_ref, s_ref, z_ref, xb_ref):
    j = pl.program_id(1)

    @pl.when(j == 0)
    def _():
        xb_ref[...] = x_ref[...].astype(BF16)

    acc = jnp.dot(xb_ref[...], w_ref[...], preferred_element_type=F32)
    acc = (acc + b_ref[...]) * s_ref[...]
    gate0 = GATE_COL0 // INPROJ_TN

    @pl.when(j < gate0)
    def _():
        z_ref[...] = acc.astype(BF16)

    @pl.when(j >= gate0)
    def _():
        z_ref[...] = (1.0 / (1.0 + jnp.exp(-acc))).astype(BF16)


def _in_proj(x2d, w_bf, b_row, scale_row):
    m = x2d.shape[0]
    return pl.pallas_call(
        _inproj_kernel,
        out_shape=jax.ShapeDtypeStruct((m, IN_COLS), BF16),
        grid=(m // INPROJ_TM, IN_COLS // INPROJ_TN),
        in_specs=[
            pl.BlockSpec((INPROJ_TM, D_MODEL), lambda i, j: (i, 0)),
            pl.BlockSpec((D_MODEL, INPROJ_TN), lambda i, j: (0, j)),
            pl.BlockSpec((1, INPROJ_TN), lambda i, j: (0, j)),
            pl.BlockSpec((1, INPROJ_TN), lambda i, j: (0, j)),
        ],
        out_specs=pl.BlockSpec((INPROJ_TM, INPROJ_TN), lambda i, j: (i, j)),
        scratch_shapes=[pltpu.VMEM((INPROJ_TM, D_MODEL), BF16)],
        compiler_params=_cparams(2, 48),
        name="in_proj",
    )(x2d, w_bf, b_row, scale_row)


def _diff_attn_kernel(slopes_ref, lq1_ref, lk1_ref, lq2_ref, lk2_ref, g_ref,
                      q_ref, k_ref, v_ref, o_ref, bias_ref):
    h = pl.program_id(0)
    qi = pl.program_id(1)
    b = pl.program_id(2)

    @pl.when(b == 0)
    def _():
        rows = lax.broadcasted_iota(jnp.int32, (DIFF_TQ, SEQ), 0) + qi * DIFF_TQ
        cols = lax.broadcasted_iota(jnp.int32, (DIFF_TQ, SEQ), 1)
        dist = jnp.abs(rows - cols).astype(F32)
        bias_ref[...] = dist * (-LOG2E * slopes_ref[h])

    lam = (jnp.exp(jnp.sum(lq1_ref[...] * lk1_ref[...], axis=-1, keepdims=True))
           - jnp.exp(jnp.sum(lq2_ref[...] * lk2_ref[...], axis=-1, keepdims=True))
           + LAMBDA_INIT)

    q = q_ref[...]
    k = k_ref[...]
    lane = lax.broadcasted_iota(jnp.int32, q.shape, 1)
    zero = jnp.zeros_like(q)
    q1 = jnp.where(lane < DIFF_QK_DIM, q, zero)
    q2 = jnp.where(lane >= DIFF_QK_DIM, q, zero)
    nt = (((1,), (1,)), ((), ()))
    bias = bias_ref[...]
    s1 = lax.dot_general(q1, k, nt, preferred_element_type=F32) + bias
    s2 = lax.dot_general(q2, k, nt, preferred_element_type=F32) + bias
    p1 = jnp.exp2(s1 - jnp.max(s1, axis=-1, keepdims=True))
    p2 = jnp.exp2(s2 - jnp.max(s2, axis=-1, keepdims=True))
    r1 = 1.0 / jnp.sum(p1, axis=-1, keepdims=True)
    r2 = lam / jnp.sum(p2, axis=-1, keepdims=True)
    a = (p1 * r1 - p2 * r2).astype(BF16)
    o = jnp.dot(a, v_ref[...], preferred_element_type=F32)
    ms = jnp.mean(o * o, axis=-1, keepdims=True)
    o = o * lax.rsqrt(ms + RMS_EPS) * g_ref[...] * (1.0 - LAMBDA_INIT)
    o_ref[...] = o.astype(BF16)


def _diff_attn(z3, slopes, lq1, lk1, lq2, lk2, subln_g):
    bsz = z3.shape[0]
    vec = lambda n: pl.BlockSpec((1, n), lambda h, qi, b: (0, 0))
    return pl.pallas_call(
        _diff_attn_kernel,
        out_shape=jax.ShapeDtypeStruct((bsz, SEQ, DIFF_WIDTH), BF16),
        grid=(N_DIFF_HEADS, SEQ // DIFF_TQ, bsz),
        in_specs=[
            pl.BlockSpec(memory_space=pltpu.SMEM),
            vec(DIFF_QK_DIM), vec(DIFF_QK_DIM), vec(DIFF_QK_DIM), vec(DIFF_QK_DIM),
            vec(DIFF_V_DIM),
            pl.BlockSpec((None, DIFF_TQ, HEAD_COLS), lambda h, qi, b: (b, qi, QA_BLK + h)),
            pl.BlockSpec((None, SEQ, HEAD_COLS), lambda h, qi, b: (b, 0, KA_BLK + h)),
            pl.BlockSpec((None, SEQ, HEAD_COLS), lambda h, qi, b: (b, 0, VA_BLK + h)),
        ],
        out_specs=pl.BlockSpec((None, DIFF_TQ, HEAD_COLS), lambda h, qi, b: (b, qi, h)),
        scratch_shapes=[pltpu.VMEM((DIFF_TQ, SEQ), F32)],
        compiler_params=_cparams(3, 48),
        name="diff_attn",
    )(slopes, lq1, lk1, lq2, lk2, subln_g, z3, z3, z3)


def _na_key_row0(g):
    lo = g * NA_QROWS - NA_KR // 2
    return jnp.clip(lo, 0, GRID_ROWS - NA_KROWS) if not isinstance(g, int) else min(max(lo, 0), GRID_ROWS - NA_KROWS)


def _na_bias_tables(rpb):
    tabs = []
    for g in (0, 1, NA_GROUPS - 1):
        qi = np.arange(NA_TQ)
        kj = np.arange(NA_TK)
        r = g * NA_QROWS + qi // GRID_W
        c = qi % GRID_W
        kr = _na_key_row0(int(g)) + kj // GRID_W
        kc = kj % GRID_W
        row_start = np.clip(r - NA_KR // 2, 0, GRID_ROWS - NA_KR)
        col_start = np.clip(c - NA_KC // 2, 0, GRID_W - NA_KC)
        valid = ((kr[None, :] >= row_start[:, None]) & (kr[None, :] < row_start[:, None] + NA_KR)
                 & (kc[None, :] >= col_start[:, None]) & (kc[None, :] < col_start[:, None] + NA_KC))
        dr = np.clip(kr[None, :] - r[:, None] + (NA_KR - 1), 0, 2 * NA_KR - 2)
        dc = np.clip(kc[None, :] - c[:, None] + (NA_KC - 1), 0, 2 * NA_KC - 2)
        flat = (dr * (2 * NA_KC - 1) + dc).astype(np.int32)
        vals = jnp.take(rpb.reshape(N_NA_HEADS, -1), jnp.asarray(flat.reshape(-1)), axis=1)
        vals = vals.reshape(N_NA_HEADS, NA_TQ, NA_TK).astype(F32) * LOG2E
        tabs.append(jnp.where(jnp.asarray(valid)[None], vals, -jnp.inf))
    return jnp.stack(tabs, axis=1)


def _na_kernel(bias_ref, q_ref, k_ref, v_ref, o_ref):
    g = pl.program_id(2)
    sel = jnp.where(g == 0, 0, jnp.where(g == NA_GROUPS - 1, 2, 1))
    start = pl.multiple_of(_na_key_row0(g) * GRID_W, GRID_W)
    k = k_ref[pl.ds(start, NA_TK), :]
    v = v_ref[pl.ds(start, NA_TK), :]
    nt = (((1,), (1,)), ((), ()))
    s = lax.dot_general(q_ref[...], k, nt, preferred_element_type=F32) + bias_ref[sel]
    p = jnp.exp2(s - jnp.max(s, axis=-1, keepdims=True))
    r = 1.0 / jnp.sum(p, axis=-1, keepdims=True)
    o = jnp.dot((p * r).astype(BF16), v, preferred_element_type=F32)
    o_ref[...] = o.astype(BF16)


def _na_attn(z3, bias_tabs):
    bsz = z3.shape[0]
    return pl.pallas_call(
        _na_kernel,
        out_shape=jax.ShapeDtypeStruct((bsz, SEQ, NA_WIDTH), BF16),
        grid=(N_NA_HEADS, bsz, NA_GROUPS),
        in_specs=[
            pl.BlockSpec((None, 3, NA_TQ, NA_TK), lambda h, b, g: (h, 0, 0, 0)),
            pl.BlockSpec((None, NA_TQ, HEAD_COLS), lambda h, b, g: (b, g, QB_BLK + h)),
            pl.BlockSpec((None, SEQ, HEAD_COLS), lambda h, b, g: (b, 0, KB_BLK + h)),
            pl.BlockSpec((None, SEQ, HEAD_COLS), lambda h, b, g: (b, 0, VB_BLK + h)),
        ],
        out_specs=pl.BlockSpec((None, NA_TQ, HEAD_COLS), lambda h, b, g: (b, g, h)),
        compiler_params=_cparams(3, 32),
        name="na_attn",
    )(bias_tabs, z3, z3, z3)


def _layer_norm(r, g, b):
    mu = jnp.mean(r, axis=-1, keepdims=True)
    d = r - mu
    var = jnp.mean(d * d, axis=-1, keepdims=True)
    return d * lax.rsqrt(var + LN_EPS) * g + b


def _merge_kernel(oa_ref, ob_ref, ga_ref, gb_ref, x_ref, wa_ref, wb_ref, wo_ref, g_ref, b_ref,
                  x1_ref, x1b_ref):
    ya = jnp.dot(oa_ref[...], wa_ref[...], preferred_element_type=F32)
    yb = jnp.dot(ob_ref[...], wb_ref[...], preferred_element_type=F32)
    y = ga_ref[...].astype(F32) * ya + gb_ref[...].astype(F32) * yb
    hproj = jnp.dot(y.astype(BF16), wo_ref[...], preferred_element_type=F32)
    x1 = _layer_norm(DEEPNORM_ALPHA * x_ref[...] + hproj, g_ref[...], b_ref[...])
    x1_ref[...] = x1
    x1b_ref[...] = x1.astype(BF16)


def _merge(oa2d, ob2d, z2d, x2d, wa, wb, wo, ln_g, ln_b):
    m = x2d.shape[0]
    gate_blk = GATE_COL0 // D_MODEL
    const = lambda shape: pl.BlockSpec(shape, lambda i: (0, 0), pipeline_mode=pl.Buffered(1))
    return pl.pallas_call(
        _merge_kernel,
        out_shape=(jax.ShapeDtypeStruct((m, D_MODEL), F32), jax.ShapeDtypeStruct((m, D_MODEL), BF16)),
        grid=(m // MERGE_TM,),
        in_specs=[
            pl.BlockSpec((MERGE_TM, DIFF_WIDTH), lambda i: (i, 0)),
            pl.BlockSpec((MERGE_TM, NA_WIDTH), lambda i: (i, 0)),
            pl.BlockSpec((MERGE_TM, D_MODEL), lambda i: (i, gate_blk)),
            pl.BlockSpec((MERGE_TM, D_MODEL), lambda i: (i, gate_blk + 1)),
            pl.BlockSpec((MERGE_TM, D_MODEL), lambda i: (i, 0)),
            const((DIFF_WIDTH, D_MODEL)),
            const((NA_WIDTH, D_MODEL)),
            const((D_MODEL, D_MODEL)),
            const((1, D_MODEL)),
            const((1, D_MODEL)),
        ],
        out_specs=(pl.BlockSpec((MERGE_TM, D_MODEL), lambda i: (i, 0)),
                   pl.BlockSpec((MERGE_TM, D_MODEL), lambda i: (i, 0))),
        compiler_params=_cparams(1, 56),
        name="merge",
    )(oa2d, ob2d, z2d, z2d, x2d, wa, wb, wo, ln_g, ln_b)


def _seq_conv(hmat, cw_ref, cb_ref):
    n = hmat.shape[0]
    row = lax.broadcasted_iota(jnp.int32, hmat.shape, 0)
    prev = jnp.where(row == 0, 0.0, pltpu.roll(hmat, 1, 0))
    nxt = jnp.where(row == n - 1, 0.0, pltpu.roll(hmat, n - 1, 0))
    return cb_ref[...] + prev * cw_ref[0:1, :] + hmat * cw_ref[1:2, :] + nxt * cw_ref[2:3, :]


def _ffn_up_kernel(x_ref, wg_ref, wv_ref, cwg_ref, cwv_ref, cbg_ref, cbv_ref, o_ref):
    x = x_ref[...]
    gate = _seq_conv(jnp.dot(x, wg_ref[...], preferred_element_type=F32), cwg_ref, cbg_ref)
    val = _seq_conv(jnp.dot(x, wv_ref[...], preferred_element_type=F32), cwv_ref, cbv_ref)
    gelu = 0.5 * gate * (1.0 + lax.erf(gate * (2.0 ** -0.5)))
    o_ref[...] = (gelu * val).astype(BF16)


def _ffn_up(x1b, w_up, conv_w, conv_b):
    m = x1b.shape[0]
    return pl.pallas_call(
        _ffn_up_kernel,
        out_shape=jax.ShapeDtypeStruct((m, D_FF), BF16),
        grid=(m // SEQ, FFN_NJ),
        in_specs=[
            pl.BlockSpec((SEQ, D_MODEL), lambda b, j: (b, 0)),
            pl.BlockSpec((D_MODEL, FFN_TN), lambda b, j: (0, j)),
            pl.BlockSpec((D_MODEL, FFN_TN), lambda b, j: (0, FFN_NJ + j)),
            pl.BlockSpec((3, FFN_TN), lambda b, j: (0, j)),
            pl.BlockSpec((3, FFN_TN), lambda b, j: (0, FFN_NJ + j)),
            pl.BlockSpec((1, FFN_TN), lambda b, j: (0, j)),
            pl.BlockSpec((1, FFN_TN), lambda b, j: (0, FFN_NJ + j)),
        ],
        out_specs=pl.BlockSpec((SEQ, FFN_TN), lambda b, j: (b, j)),
        compiler_params=_cparams(2, 56),
        name="ffn_up",
    )(x1b, w_up, w_up, conv_w, conv_w, conv_b, conv_b)


def _ffn_down_kernel(g_ref, w_ref, x1_ref, lg_ref, lb_ref, o_ref, acc_ref):
    kk = pl.program_id(1)

    @pl.when(kk == 0)
    def _():
        acc_ref[...] = DEEPNORM_ALPHA * x1_ref[...]

    acc_ref[...] += jnp.dot(g_ref[...], w_ref[...], preferred_element_type=F32)

    @pl.when(kk == pl.num_programs(1) - 1)
    def _():
        o_ref[...] = _layer_norm(acc_ref[...], lg_ref[...], lb_ref[...])


def _ffn_down(g2d, w_down, x1, ln_g, ln_b):
    m = g2d.shape[0]
    return pl.pallas_call(
        _ffn_down_kernel,
        out_shape=jax.ShapeDtypeStruct((m, D_MODEL), F32),
        grid=(m // DOWN_TM, D_FF // DOWN_TK),
        in_specs=[
            pl.BlockSpec((DOWN_TM, DOWN_TK), lambda i, k: (i, k)),
            pl.BlockSpec((DOWN_TK, D_MODEL), lambda i, k: (k, 0)),
            pl.BlockSpec((DOWN_TM, D_MODEL), lambda i, k: (i, 0)),
            pl.BlockSpec((1, D_MODEL), lambda i, k: (0, 0)),
            pl.BlockSpec((1, D_MODEL), lambda i, k: (0, 0)),
        ],
        out_specs=pl.BlockSpec((DOWN_TM, D_MODEL), lambda i, k: (i, 0)),
        scratch_shapes=[pltpu.VMEM((DOWN_TM, D_MODEL), F32)],
        compiler_params=_cparams(2, 56),
        name="ffn_down",
    )(g2d, w_down, x1, ln_g, ln_b)


def _z_col_scale():
    s = np.ones((1, IN_COLS), np.float32)
    s[0, :DIFF_WIDTH] = DIFF_QK_DIM ** -0.5 * LOG2E
    s[0, 3 * DIFF_WIDTH:3 * DIFF_WIDTH + NA_WIDTH] = NA_HEAD_DIM ** -0.5 * LOG2E
    return s


def _alibi_slopes():
    n = N_DIFF_HEADS
    return np.array([2.0 ** (-8.0 * (i + 1) / n) for i in range(n)], dtype=np.float32)


def kernel(x, w_in, b_in, lam_q1, lam_k1, lam_q2, lam_k2, subln_g, rpb, w_branch_a, w_branch_b, w_out,
           ln1_g, ln1_b, w_up, conv_w, conv_b, w_down, ln2_g, ln2_b):
    bsz, seq, d = x.shape
    assert (seq, d) == (SEQ, D_MODEL) and w_in.shape[0] == 1
    row = lambda a: a.reshape(1, -1).astype(F32)
    x2d = x.reshape(bsz * seq, d)

    z = _in_proj(x2d, w_in[0].astype(BF16), row(b_in[0]), jnp.asarray(_z_col_scale()))
    z3 = z.reshape(bsz, seq, IN_COLS)
    oa = _diff_attn(z3, jnp.asarray(_alibi_slopes()), row(lam_q1[0]), row(lam_k1[0]), row(lam_q2[0]),
                    row(lam_k2[0]), row(subln_g[0]))
    ob = _na_attn(z3, _na_bias_tables(rpb[0]))
    x1, x1b = _merge(oa.reshape(bsz * seq, DIFF_WIDTH), ob.reshape(bsz * seq, NA_WIDTH), z, x2d,
                     w_branch_a[0].astype(BF16), w_branch_b[0].astype(BF16), w_out[0].astype(BF16),
                     row(ln1_g[0]), row(ln1_b[0]))
    g = _ffn_up(x1b, w_up[0].astype(BF16), conv_w[0].astype(F32), row(conv_b[0]))
    out = _ffn_down(g, w_down[0].astype(BF16), x1, row(ln2_g[0]), row(ln2_b[0]))
    return out.reshape(bsz, seq, d)
```

```python
import math
from functools import partial

import jax
import jax.numpy as jnp
import numpy as np
from jax import lax
from jax.experimental import pallas as pl
from jax.experimental.pallas import tpu as pltpu

F32 = jnp.float32
BF16 = jnp.bfloat16

D_MODEL = 2048
SEQ = 2048
GRID_W = 64
GRID_ROWS = SEQ // GRID_W
N_DIFF_HEADS = 8
DIFF_QK_DIM = 64
DIFF_V_DIM = 2 * DIFF_QK_DIM
DIFF_WIDTH = N_DIFF_HEADS * DIFF_V_DIM
N_NA_HEADS = 8
NA_HEAD_DIM = 128
NA_WIDTH = N_NA_HEADS * NA_HEAD_DIM
NA_KR = 8
NA_KC = 16
IN_COLS = 3 * DIFF_WIDTH + 3 * NA_WIDTH + 2 * D_MODEL
D_FF = 5632
LN_EPS = 1e-5
RMS_EPS = 1e-5
DEEPNORM_ALPHA = 2.0 ** 0.25
LAMBDA_INIT = 0.8 - 0.6 * math.exp(-0.3 * 0)
LOG2E = 1.4426950408889634

HEAD_COLS = 128
QA_BLK, KA_BLK, VA_BLK = 0, 8, 16
QB_BLK, KB_BLK, VB_BLK = 24, 32, 40
GATE_COL0 = 3 * DIFF_WIDTH + 3 * NA_WIDTH

MIB = 1024 * 1024
INPROJ_TM, INPROJ_TN = 1024, 1024
DIFF_TQ = 256
NA_QROWS = 4
NA_KROWS = 12
NA_TQ = NA_QROWS * GRID_W
NA_TK = NA_KROWS * GRID_W
NA_GROUPS = GRID_ROWS // NA_QROWS
MERGE_TM = 512
FFN_TN = 512
FFN_NJ = D_FF // FFN_TN
DOWN_TM, DOWN_TK = 1024, 512


def _cparams(n_axes, vmem_mib):
    return pltpu.CompilerParams(
        dimension_semantics=("arbitrary",) * n_axes,
        vmem_limit_bytes=vmem_mib * MIB,
    )


def _inproj_kernel(x_ref, w_ref, b_ref, s_ref, z_ref, xb_ref):
    j = pl.program_id(1)

    @pl.when(j == 0)
    def _():
        xb_ref[...] = x_ref[...].astype(BF16)

    acc = jnp.dot(xb_ref[...], w_ref[...], preferred_element_type=F32)
    acc = (acc + b_ref[...]) * s_ref[...]
    gate0 = GATE_COL0 // INPROJ_TN

    @pl.when(j < gate0)
    def _():
        z_ref[...] = acc.astype(BF16)

    @pl.when(j >= gate0)
    def _():
        z_ref[...] = (1.0 / (1.0 + jnp.exp(-acc))).astype(BF16)


def _in_proj(x2d, w_bf, b_row, scale_row):
    m = x2d.shape[0]
    return pl.pallas_call(
        _inproj_kernel,
        out_shape=jax.ShapeDtypeStruct((m, IN_COLS), BF16),
        grid=(m // INPROJ_TM, IN_COLS // INPROJ_TN),
        in_specs=[
            pl.BlockSpec((INPROJ_TM, D_MODEL), lambda i, j: (i, 0)),
            pl.BlockSpec((D_MODEL, INPROJ_TN), lambda i, j: (0, j)),
            pl.BlockSpec((1, INPROJ_TN), lambda i, j: (0, j)),
            pl.BlockSpec((1, INPROJ_TN), lambda i, j: (0, j)),
        ],
        out_specs=pl.BlockSpec((INPROJ_TM, INPROJ_TN), lambda i, j: (i, j)),
        scratch_shapes=[pltpu.VMEM((INPROJ_TM, D_MODEL), BF16)],
        compiler_params=_cparams(2, 48),
        name="in_proj",
    )(x2d, w_bf, b_row, scale_row)


def _diff_attn_kernel(slopes_ref, lq1_ref, lk1_ref, lq2_ref, lk2_ref, g_ref,
                      q_ref, k_ref, v_ref, o_ref, bias_ref):
    h = pl.program_id(0)
    qi = pl.program_id(1)
    b = pl.program_id(2)

    @pl.when(b == 0)
    def _():
        rows = lax.broadcasted_iota(jnp.int32, (DIFF_TQ, SEQ), 0) + qi * DIFF_TQ
        cols = lax.broadcasted_iota(jnp.int32, (DIFF_TQ, SEQ), 1)
        dist = jnp.abs(rows - cols).astype(F32)
        bias_ref[...] = dist * (-LOG2E * slopes_ref[h])

    lam = (jnp.exp(jnp.sum(lq1_ref[...] * lk1_ref[...], axis=-1, keepdims=True))
           - jnp.exp(jnp.sum(lq2_ref[...] * lk2_ref[...], axis=-1, keepdims=True))
           + LAMBDA_INIT)

    q = q_ref[...]
    k = k_ref[...]
    lane = lax.broadcasted_iota(jnp.int32, q.shape, 1)
    zero = jnp.zeros_like(q)
    q1 = jnp.where(lane < DIFF_QK_DIM, q, zero)
    q2 = jnp.where(lane >= DIFF_QK_DIM, q, zero)
    nt = (((1,), (1,)), ((), ()))
    bias = bias_ref[...]
    s1 = lax.dot_general(q1, k, nt, preferred_element_type=F32) + bias
    s2 = lax.dot_general(q2, k, nt, preferred_element_type=F32) + bias
    p1 = jnp.exp2(s1 - jnp.max(s1, axis=-1, keepdims=True))
    p2 = jnp.exp2(s2 - jnp.max(s2, axis=-1, keepdims=True))
    r1 = 1.0 / jnp.sum(p1, axis=-1, keepdims=True)
    r2 = lam / jnp.sum(p2, axis=-1, keepdims=True)
    a = (p1 * r1 - p2 * r2).astype(BF16)
    o = jnp.dot(a, v_ref[...], preferred_element_type=F32)
    ms = jnp.mean(o * o, axis=-1, keepdims=True)
    o = o * lax.rsqrt(ms + RMS_EPS) * g_ref[...] * (1.0 - LAMBDA_INIT)
    o_ref[...] = o.astype(BF16)


def _diff_attn(z3, slopes, lq1, lk1, lq2, lk2, subln_g):
    bsz = z3.shape[0]
    vec = lambda n: pl.BlockSpec((1, n), lambda h, qi, b: (0, 0))
    return pl.pallas_call(
        _diff_attn_kernel,
        out_shape=jax.ShapeDtypeStruct((bsz, SEQ, DIFF_WIDTH), BF16),
        grid=(N_DIFF_HEADS, SEQ // DIFF_TQ, bsz),
        in_specs=[
            pl.BlockSpec(memory_space=pltpu.SMEM),
            vec(DIFF_QK_DIM), vec(DIFF_QK_DIM), vec(DIFF_QK_DIM), vec(DIFF_QK_DIM),
            vec(DIFF_V_DIM),
            pl.BlockSpec((None, DIFF_TQ, HEAD_COLS), lambda h, qi, b: (b, qi, QA_BLK + h)),
            pl.BlockSpec((None, SEQ, HEAD_COLS), lambda h, qi, b: (b, 0, KA_BLK + h)),
            pl.BlockSpec((None, SEQ, HEAD_COLS), lambda h, qi, b: (b, 0, VA_BLK + h)),
        ],
        out_specs=pl.BlockSpec((None, DIFF_TQ, HEAD_COLS), lambda h, qi, b: (b, qi, h)),
        scratch_shapes=[pltpu.VMEM((DIFF_TQ, SEQ), F32)],
        compiler_params=_cparams(3, 48),
        name="diff_attn",
    )(slopes, lq1, lk1, lq2, lk2, subln_g, z3, z3, z3)


def _na_key_row0(g):
    return min(max(g * NA_QROWS - NA_KR // 2, 0), GRID_ROWS - NA_KROWS)


def _na_table_id(g):
    return 0 if g == 0 else (2 if g == NA_GROUPS - 1 else 1)


def _na_bias_tables(rpb):
    n_dr, n_dc = 2 * NA_KR - 1, 2 * NA_KC - 1
    period = 2 * GRID_W
    w = jnp.pad(rpb.astype(F32), ((0, 0), (0, 0), (0, period - n_dc)))
    f = jnp.tile(w, (1, 1, GRID_W))[..., :GRID_W * (period - 1)]
    toe = f.reshape(N_NA_HEADS, n_dr, GRID_W, period - 1)[..., NA_KC - 1:NA_KC - 1 + GRID_W]
    toe = jnp.pad(toe.transpose(0, 2, 1, 3), ((0, 0), (0, 0), (NA_KROWS, NA_KROWS), (0, 0)))
    tabs = []
    for g in (0, 1, NA_GROUPS - 1):
        strips = []
        for rq in range(NA_QROWS):
            off = _na_key_row0(g) - (g * NA_QROWS + rq) + (NA_KR - 1) + NA_KROWS
            strips.append(toe[:, :, off:off + NA_KROWS, :].reshape(N_NA_HEADS, GRID_W, NA_TK))
        vals = jnp.stack(strips, axis=1).reshape(N_NA_HEADS, NA_TQ, NA_TK) * LOG2E
        qi, kj = np.arange(NA_TQ), np.arange(NA_TK)
        r, c = g * NA_QROWS + qi // GRID_W, qi % GRID_W
        kr, kc = _na_key_row0(g) + kj // GRID_W, kj % GRID_W
        row_start = np.clip(r - NA_KR // 2, 0, GRID_ROWS - NA_KR)
        col_start = np.clip(c - NA_KC // 2, 0, GRID_W - NA_KC)
        valid = ((kr[None, :] >= row_start[:, None]) & (kr[None, :] < row_start[:, None] + NA_KR)
                 & (kc[None, :] >= col_start[:, None]) & (kc[None, :] < col_start[:, None] + NA_KC))
        tabs.append(jnp.where(jnp.asarray(valid)[None], vals, -jnp.inf))
    return jnp.stack(tabs, axis=1)


def _na_kernel(bias_ref, q_ref, k_ref, v_ref, o_ref):
    nt = (((1,), (1,)), ((), ()))
    for g in range(NA_GROUPS):
        rows = pl.ds(g * NA_TQ, NA_TQ)
        keys = pl.ds(_na_key_row0(g) * GRID_W, NA_TK)
        s = lax.dot_general(q_ref[rows, :], k_ref[keys, :], nt, preferred_element_type=F32)
        s = s + bias_ref[_na_table_id(g)]
        p = jnp.exp2(s - jnp.max(s, axis=-1, keepdims=True))
        r = 1.0 / jnp.sum(p, axis=-1, keepdims=True)
        o = jnp.dot(p.astype(BF16), v_ref[keys, :], preferred_element_type=F32) * r
        o_ref[rows, :] = o.astype(BF16)


def _na_attn(z3, bias_tabs):
    bsz = z3.shape[0]
    head_block = lambda blk: pl.BlockSpec((None, SEQ, HEAD_COLS), lambda h, b: (b, 0, blk + h))
    return pl.pallas_call(
        _na_kernel,
        out_shape=jax.ShapeDtypeStruct((bsz, SEQ, NA_WIDTH), BF16),
        grid=(N_NA_HEADS, bsz),
        in_specs=[
            pl.BlockSpec((None, 3, NA_TQ, NA_TK), lambda h, b: (h, 0, 0, 0)),
            head_block(QB_BLK), head_block(KB_BLK), head_block(VB_BLK),
        ],
        out_specs=head_block(0),
        compiler_params=_cparams(2, 32),
        name="na_attn",
    )(bias_tabs, z3, z3, z3)


def _layer_norm(r, g, b):
    mu = jnp.mean(r, axis=-1, keepdims=True)
    d = r - mu
    var = jnp.mean(d * d, axis=-1, keepdims=True)
    return d * lax.rsqrt(var + LN_EPS) * g + b


def _merge_kernel(oa_ref, ob_ref, ga_ref, gb_ref, x_ref, wa_ref, wb_ref, wo_ref, g_ref, b_ref,
                  x1_ref, x1b_ref):
    ya = jnp.dot(oa_ref[...], wa_ref[...], preferred_element_type=F32)
    yb = jnp.dot(ob_ref[...], wb_ref[...], preferred_element_type=F32)
    y = ga_ref[...].astype(F32) * ya + gb_ref[...].astype(F32) * yb
    hproj = jnp.dot(y.astype(BF16), wo_ref[...], preferred_element_type=F32)
    x1 = _layer_norm(DEEPNORM_ALPHA * x_ref[...] + hproj, g_ref[...], b_ref[...])
    x1_ref[...] = x1
    x1b_ref[...] = x1.astype(BF16)


def _merge(oa2d, ob2d, z2d, x2d, wa, wb, wo, ln_g, ln_b):
    m = x2d.shape[0]
    gate_blk = GATE_COL0 // D_MODEL
    const = lambda shape: pl.BlockSpec(shape, lambda i: (0, 0), pipeline_mode=pl.Buffered(1))
    return pl.pallas_call(
        _merge_kernel,
        out_shape=(jax.ShapeDtypeStruct((m, D_MODEL), F32), jax.ShapeDtypeStruct((m, D_MODEL), BF16)),
        grid=(m // MERGE_TM,),
        in_specs=[
            pl.BlockSpec((MERGE_TM, DIFF_WIDTH), lambda i: (i, 0)),
            pl.BlockSpec((MERGE_TM, NA_WIDTH), lambda i: (i, 0)),
            pl.BlockSpec((MERGE_TM, D_MODEL), lambda i: (i, gate_blk)),
            pl.BlockSpec((MERGE_TM, D_MODEL), lambda i: (i, gate_blk + 1)),
            pl.BlockSpec((MERGE_TM, D_MODEL), lambda i: (i, 0)),
            const((DIFF_WIDTH, D_MODEL)),
            const((NA_WIDTH, D_MODEL)),
            const((D_MODEL, D_MODEL)),
            const((1, D_MODEL)),
            const((1, D_MODEL)),
        ],
        out_specs=(pl.BlockSpec((MERGE_TM, D_MODEL), lambda i: (i, 0)),
                   pl.BlockSpec((MERGE_TM, D_MODEL), lambda i: (i, 0))),
        compiler_params=_cparams(1, 56),
        name="merge",
    )(oa2d, ob2d, z2d, z2d, x2d, wa, wb, wo, ln_g, ln_b)


def _seq_conv(hmat, cw_ref, cb_ref):
    n = hmat.shape[0]
    row = lax.broadcasted_iota(jnp.int32, hmat.shape, 0)
    prev = jnp.where(row == 0, 0.0, pltpu.roll(hmat, 1, 0))
    nxt = jnp.where(row == n - 1, 0.0, pltpu.roll(hmat, n - 1, 0))
    return cb_ref[...] + prev * cw_ref[0:1, :] + hmat * cw_ref[1:2, :] + nxt * cw_ref[2:3, :]


def _ffn_up_kernel(x_ref, wg_ref, wv_ref, cwg_ref, cwv_ref, cbg_ref, cbv_ref, o_ref):
    x = x_ref[...]
    gate = _seq_conv(jnp.dot(x, wg_ref[...], preferred_element_type=F32), cwg_ref, cbg_ref)
    val = _seq_conv(jnp.dot(x, wv_ref[...], preferred_element_type=F32), cwv_ref, cbv_ref)
    gelu = 0.5 * gate * (1.0 + lax.erf(gate * (2.0 ** -0.5)))
    o_ref[...] = (gelu * val).astype(BF16)


def _ffn_up(x1b, w_up, conv_w, conv_b):
    m = x1b.shape[0]
    return pl.pallas_call(
        _ffn_up_kernel,
        out_shape=jax.ShapeDtypeStruct((m, D_FF), BF16),
        grid=(m // SEQ, FFN_NJ),
        in_specs=[
            pl.BlockSpec((SEQ, D_MODEL), lambda b, j: (b, 0)),
            pl.BlockSpec((D_MODEL, FFN_TN), lambda b, j: (0, j)),
            pl.BlockSpec((D_MODEL, FFN_TN), lambda b, j: (0, FFN_NJ + j)),
            pl.BlockSpec((3, FFN_TN), lambda b, j: (0, j)),
            pl.BlockSpec((3, FFN_TN), lambda b, j: (0, FFN_NJ + j)),
            pl.BlockSpec((1, FFN_TN), lambda b, j: (0, j)),
            pl.BlockSpec((1, FFN_TN), lambda b, j: (0, FFN_NJ + j)),
        ],
        out_specs=pl.BlockSpec((SEQ, FFN_TN), lambda b, j: (b, j)),
        compiler_params=_cparams(2, 56),
        name="ffn_up",
    )(x1b, w_up, w_up, conv_w, conv_w, conv_b, conv_b)


def _ffn_down_kernel(g_ref, w_ref, x1_ref, lg_ref, lb_ref, o_ref, acc_ref):
    kk = pl.program_id(1)

    @pl.when(kk == 0)
    def _():
        acc_ref[...] = DEEPNORM_ALPHA * x1_ref[...]

    acc_ref[...] += jnp.dot(g_ref[...], w_ref[...], preferred_element_type=F32)

    @pl.when(kk == pl.num_programs(1) - 1)
    def _():
        o_ref[...] = _layer_norm(acc_ref[...], lg_ref[...], lb_ref[...])


def _ffn_down(g2d, w_down, x1, ln_g, ln_b):
    m = g2d.shape[0]
    return pl.pallas_call(
        _ffn_down_kernel,
        out_shape=jax.ShapeDtypeStruct((m, D_MODEL), F32),
        grid=(m // DOWN_TM, D_FF // DOWN_TK),
        in_specs=[
            pl.BlockSpec((DOWN_TM, DOWN_TK), lambda i, k: (i, k)),
            pl.BlockSpec((DOWN_TK, D_MODEL), lambda i, k: (k, 0)),
            pl.BlockSpec((DOWN_TM, D_MODEL), lambda i, k: (i, 0)),
            pl.BlockSpec((1, D_MODEL), lambda i, k: (0, 0)),
            pl.BlockSpec((1, D_MODEL), lambda i, k: (0, 0)),
        ],
        out_specs=pl.BlockSpec((DOWN_TM, D_MODEL), lambda i, k: (i, 0)),
        scratch_shapes=[pltpu.VMEM((DOWN_TM, D_MODEL), F32)],
        compiler_params=_cparams(2, 56),
        name="ffn_down",
    )(g2d, w_down, x1, ln_g, ln_b)


def _z_col_scale():
    s = np.ones((1, IN_COLS), np.float32)
    s[0, :DIFF_WIDTH] = DIFF_QK_DIM ** -0.5 * LOG2E
    s[0, 3 * DIFF_WIDTH:3 * DIFF_WIDTH + NA_WIDTH] = NA_HEAD_DIM ** -0.5 * LOG2E
    return s


def _alibi_slopes():
    n = N_DIFF_HEADS
    return np.array([2.0 ** (-8.0 * (i + 1) / n) for i in range(n)], dtype=np.float32)


def kernel(x, w_in, b_in, lam_q1, lam_k1, lam_q2, lam_k2, subln_g, rpb, w_branch_a, w_branch_b, w_out,
           ln1_g, ln1_b, w_up, conv_w, conv_b, w_down, ln2_g, ln2_b):
    bsz, seq, d = x.shape
    assert (seq, d) == (SEQ, D_MODEL) and w_in.shape[0] == 1
    row = lambda a: a.reshape(1, -1).astype(F32)
    x2d = x.reshape(bsz * seq, d)

    z = _in_proj(x2d, w_in[0].astype(BF16), row(b_in[0]), jnp.asarray(_z_col_scale()))
    z3 = z.reshape(bsz, seq, IN_COLS)
    oa = _diff_attn(z3, jnp.asarray(_alibi_slopes()), row(lam_q1[0]), row(lam_k1[0]), row(lam_q2[0]),
                    row(lam_k2[0]), row(subln_g[0]))
    ob = _na_attn(z3, _na_bias_tables(rpb[0]))
    x1, x1b = _merge(oa.reshape(bsz * seq, DIFF_WIDTH), ob.reshape(bsz * seq, NA_WIDTH), z, x2d,
                     w_branch_a[0].astype(BF16), w_branch_b[0].astype(BF16), w_out[0].astype(BF16),
                     row(ln1_g[0]), row(ln1_b[0]))
    g = _ffn_up(x1b, w_up[0].astype(BF16), conv_w[0].astype(F32), row(conv_b[0]))
    out = _ffn_down(g, w_down[0].astype(BF16), x1, row(ln2_g[0]), row(ln2_b[0]))
    return out.reshape(bsz, seq, d)
```

```python
import math
from functools import partial

import jax
import jax.numpy as jnp
import numpy as np
from jax import lax
from jax.experimental import pallas as pl
from jax.experimental.pallas import tpu as pltpu

F32 = jnp.float32
BF16 = jnp.bfloat16

D_MODEL = 2048
SEQ = 2048
GRID_W = 64
GRID_ROWS = SEQ // GRID_W
N_DIFF_HEADS = 8
DIFF_QK_DIM = 64
DIFF_V_DIM = 2 * DIFF_QK_DIM
DIFF_WIDTH = N_DIFF_HEADS * DIFF_V_DIM
N_NA_HEADS = 8
NA_HEAD_DIM = 128
NA_WIDTH = N_NA_HEADS * NA_HEAD_DIM
NA_KR = 8
NA_KC = 16
IN_COLS = 3 * DIFF_WIDTH + 3 * NA_WIDTH + 2 * D_MODEL
D_FF = 5632
LN_EPS = 1e-5
RMS_EPS = 1e-5
DEEPNORM_ALPHA = 2.0 ** 0.25
LAMBDA_INIT = 0.8 - 0.6 * math.exp(-0.3 * 0)
LOG2E = 1.4426950408889634

HEAD_COLS = 128
QA_BLK, KA_BLK, VA_BLK = 0, 8, 16
QB_BLK, KB_BLK, VB_BLK = 24, 32, 40
GATE_COL0 = 3 * DIFF_WIDTH + 3 * NA_WIDTH

MIB = 1024 * 1024
INPROJ_TM, INPROJ_TN = 1024, 1024
DIFF_TQ = 256
NA_QROWS = 4
NA_KROWS = 12
NA_TQ = NA_QROWS * GRID_W
NA_TK = NA_KROWS * GRID_W
NA_GROUPS = GRID_ROWS // NA_QROWS
MERGE_TM = 512
FFN_TN = 512
FFN_NJ = D_FF // FFN_TN
FFN_TM = 512
F32_SUBLANES = 8
DOWN_TM, DOWN_TK = 1024, 512


def _cparams(n_axes, vmem_mib):
    return pltpu.CompilerParams(
        dimension_semantics=("arbitrary",) * n_axes,
        vmem_limit_bytes=vmem_mib * MIB,
    )


def _inproj_kernel(x_ref, w_ref, b_ref, s_ref, z_ref, xb_ref):
    j = pl.program_id(1)

    @pl.when(j == 0)
    def _():
        xb_ref[...] = x_ref[...].astype(BF16)

    acc = jnp.dot(xb_ref[...], w_ref[...], preferred_element_type=F32)
    acc = (acc + b_ref[...]) * s_ref[...]
    gate0 = GATE_COL0 // INPROJ_TN

    @pl.when(j < gate0)
    def _():
        z_ref[...] = acc.astype(BF16)

    @pl.when(j >= gate0)
    def _():
        z_ref[...] = (0.5 * jnp.tanh(0.5 * acc) + 0.5).astype(BF16)


def _in_proj(x2d, w_bf, b_row, scale_row):
    m = x2d.shape[0]
    return pl.pallas_call(
        _inproj_kernel,
        out_shape=jax.ShapeDtypeStruct((m, IN_COLS), BF16),
        grid=(m // INPROJ_TM, IN_COLS // INPROJ_TN),
        in_specs=[
            pl.BlockSpec((INPROJ_TM, D_MODEL), lambda i, j: (i, 0)),
            pl.BlockSpec((D_MODEL, INPROJ_TN), lambda i, j: (0, j)),
            pl.BlockSpec((1, INPROJ_TN), lambda i, j: (0, j)),
            pl.BlockSpec((1, INPROJ_TN), lambda i, j: (0, j)),
        ],
        out_specs=pl.BlockSpec((INPROJ_TM, INPROJ_TN), lambda i, j: (i, j)),
        scratch_shapes=[pltpu.VMEM((INPROJ_TM, D_MODEL), BF16)],
        compiler_params=_cparams(2, 48),
        name="in_proj",
    )(x2d, w_bf, b_row, scale_row)


def _diff_attn_kernel(slopes_ref, lq1_ref, lk1_ref, lq2_ref, lk2_ref, g_ref,
                      q_ref, k_ref, v_ref, o_ref, bias_ref):
    h = pl.program_id(0)
    qi = pl.program_id(1)
    b = pl.program_id(2)

    @pl.when(b == 0)
    def _():
        rows = lax.broadcasted_iota(jnp.int32, (DIFF_TQ, SEQ), 0) + qi * DIFF_TQ
        cols = lax.broadcasted_iota(jnp.int32, (DIFF_TQ, SEQ), 1)
        dist = jnp.abs(rows - cols).astype(F32)
        bias_ref[...] = dist * (-LOG2E * slopes_ref[h])

    lam = (jnp.exp(jnp.sum(lq1_ref[...] * lk1_ref[...], axis=-1, keepdims=True))
           - jnp.exp(jnp.sum(lq2_ref[...] * lk2_ref[...], axis=-1, keepdims=True))
           + LAMBDA_INIT)

    q = q_ref[...]
    k = k_ref[...]
    lane = lax.broadcasted_iota(jnp.int32, q.shape, 1)
    zero = jnp.zeros_like(q)
    q1 = jnp.where(lane < DIFF_QK_DIM, q, zero)
    q2 = jnp.where(lane >= DIFF_QK_DIM, q, zero)
    nt = (((1,), (1,)), ((), ()))
    bias = bias_ref[...]
    s1 = lax.dot_general(q1, k, nt, preferred_element_type=F32) + bias
    s2 = lax.dot_general(q2, k, nt, preferred_element_type=F32) + bias
    p1 = jnp.exp2(s1 - jnp.max(s1, axis=-1, keepdims=True))
    p2 = jnp.exp2(s2 - jnp.max(s2, axis=-1, keepdims=True))
    l1 = jnp.sum(p1, axis=-1, keepdims=True)
    l2 = jnp.sum(p2, axis=-1, keepdims=True)
    a = (p1 - p2 * (lam * l1 / l2)).astype(BF16)
    o = jnp.dot(a, v_ref[...], preferred_element_type=F32) * (1.0 / l1)
    ms = jnp.mean(o * o, axis=-1, keepdims=True)
    o = o * lax.rsqrt(ms + RMS_EPS) * g_ref[...] * (1.0 - LAMBDA_INIT)
    o_ref[...] = o.astype(BF16)


def _diff_attn(z3, slopes, lq1, lk1, lq2, lk2, subln_g):
    bsz = z3.shape[0]
    vec = lambda n: pl.BlockSpec((1, n), lambda h, qi, b: (0, 0))
    return pl.pallas_call(
        _diff_attn_kernel,
        out_shape=jax.ShapeDtypeStruct((bsz, SEQ, DIFF_WIDTH), BF16),
        grid=(N_DIFF_HEADS, SEQ // DIFF_TQ, bsz),
        in_specs=[
            pl.BlockSpec(memory_space=pltpu.SMEM),
            vec(DIFF_QK_DIM), vec(DIFF_QK_DIM), vec(DIFF_QK_DIM), vec(DIFF_QK_DIM),
            vec(DIFF_V_DIM),
            pl.BlockSpec((None, DIFF_TQ, HEAD_COLS), lambda h, qi, b: (b, qi, QA_BLK + h)),
            pl.BlockSpec((None, SEQ, HEAD_COLS), lambda h, qi, b: (b, 0, KA_BLK + h)),
            pl.BlockSpec((None, SEQ, HEAD_COLS), lambda h, qi, b: (b, 0, VA_BLK + h)),
        ],
        out_specs=pl.BlockSpec((None, DIFF_TQ, HEAD_COLS), lambda h, qi, b: (b, qi, h)),
        scratch_shapes=[pltpu.VMEM((DIFF_TQ, SEQ), F32)],
        compiler_params=_cparams(3, 48),
        name="diff_attn",
    )(slopes, lq1, lk1, lq2, lk2, subln_g, z3, z3, z3)


def _na_key_row0(g):
    return min(max(g * NA_QROWS - NA_KR // 2, 0), GRID_ROWS - NA_KROWS)


def _na_table_id(g):
    return 0 if g == 0 else (2 if g == NA_GROUPS - 1 else 1)


def _na_bias_tables(rpb):
    n_dr, n_dc = 2 * NA_KR - 1, 2 * NA_KC - 1
    period = 2 * GRID_W
    w = jnp.pad(rpb.astype(F32), ((0, 0), (0, 0), (0, period - n_dc)))
    f = jnp.tile(w, (1, 1, GRID_W))[..., :GRID_W * (period - 1)]
    toe = f.reshape(N_NA_HEADS, n_dr, GRID_W, period - 1)[..., NA_KC - 1:NA_KC - 1 + GRID_W]
    toe = jnp.pad(toe.transpose(0, 2, 1, 3), ((0, 0), (0, 0), (NA_KROWS, NA_KROWS), (0, 0)))
    tabs = []
    for g in (0, 1, NA_GROUPS - 1):
        strips = []
        for rq in range(NA_QROWS):
            off = _na_key_row0(g) - (g * NA_QROWS + rq) + (NA_KR - 1) + NA_KROWS
            strips.append(toe[:, :, off:off + NA_KROWS, :].reshape(N_NA_HEADS, GRID_W, NA_TK))
        vals = jnp.stack(strips, axis=1).reshape(N_NA_HEADS, NA_TQ, NA_TK) * LOG2E
        qi, kj = np.arange(NA_TQ), np.arange(NA_TK)
        r, c = g * NA_QROWS + qi // GRID_W, qi % GRID_W
        kr, kc = _na_key_row0(g) + kj // GRID_W, kj % GRID_W
        row_start = np.clip(r - NA_KR // 2, 0, GRID_ROWS - NA_KR)
        col_start = np.clip(c - NA_KC // 2, 0, GRID_W - NA_KC)
        valid = ((kr[None, :] >= row_start[:, None]) & (kr[None, :] < row_start[:, None] + NA_KR)
                 & (kc[None, :] >= col_start[:, None]) & (kc[None, :] < col_start[:, None] + NA_KC))
        tabs.append(jnp.where(jnp.asarray(valid)[None], vals, -jnp.inf))
    return jnp.stack(tabs, axis=1)


def _na_kernel(bias_ref, q_ref, k_ref, v_ref, o_ref):
    nt = (((1,), (1,)), ((), ()))
    for g in range(NA_GROUPS):
        rows = pl.ds(g * NA_TQ, NA_TQ)
        keys = pl.ds(_na_key_row0(g) * GRID_W, NA_TK)
        s = lax.dot_general(q_ref[rows, :], k_ref[keys, :], nt, preferred_element_type=F32)
        s = s + bias_ref[_na_table_id(g)]
        p = jnp.exp2(s - jnp.max(s, axis=-1, keepdims=True))
        r = 1.0 / jnp.sum(p, axis=-1, keepdims=True)
        o = jnp.dot(p.astype(BF16), v_ref[keys, :], preferred_element_type=F32) * r
        o_ref[rows, :] = o.astype(BF16)


def _na_attn(z3, bias_tabs):
    bsz = z3.shape[0]
    head_block = lambda blk: pl.BlockSpec((None, SEQ, HEAD_COLS), lambda h, b: (b, 0, blk + h))
    return pl.pallas_call(
        _na_kernel,
        out_shape=jax.ShapeDtypeStruct((bsz, SEQ, NA_WIDTH), BF16),
        grid=(N_NA_HEADS, bsz),
        in_specs=[
            pl.BlockSpec((None, 3, NA_TQ, NA_TK), lambda h, b: (h, 0, 0, 0)),
            head_block(QB_BLK), head_block(KB_BLK), head_block(VB_BLK),
        ],
        out_specs=head_block(0),
        compiler_params=_cparams(2, 32),
        name="na_attn",
    )(bias_tabs, z3, z3, z3)


def _layer_norm(r, g, b):
    mu = jnp.mean(r, axis=-1, keepdims=True)
    d = r - mu
    var = jnp.mean(d * d, axis=-1, keepdims=True)
    return d * lax.rsqrt(var + LN_EPS) * g + b


def _merge_kernel(oa_ref, ob_ref, ga_ref, gb_ref, x_ref, wa_ref, wb_ref, wo_ref, g_ref, b_ref,
                  x1_ref, x1b_ref):
    ya = jnp.dot(oa_ref[...], wa_ref[...], preferred_element_type=F32)
    yb = jnp.dot(ob_ref[...], wb_ref[...], preferred_element_type=F32)
    y = ga_ref[...].astype(F32) * ya + gb_ref[...].astype(F32) * yb
    hproj = jnp.dot(y.astype(BF16), wo_ref[...], preferred_element_type=F32)
    x1 = _layer_norm(DEEPNORM_ALPHA * x_ref[...] + hproj, g_ref[...], b_ref[...])
    x1_ref[...] = x1
    x1b_ref[...] = x1.astype(BF16)


def _merge(oa2d, ob2d, z2d, x2d, wa, wb, wo, ln_g, ln_b):
    m = x2d.shape[0]
    gate_blk = GATE_COL0 // D_MODEL
    const = lambda shape: pl.BlockSpec(shape, lambda i: (0, 0), pipeline_mode=pl.Buffered(1))
    return pl.pallas_call(
        _merge_kernel,
        out_shape=(jax.ShapeDtypeStruct((m, D_MODEL), F32), jax.ShapeDtypeStruct((m, D_MODEL), BF16)),
        grid=(m // MERGE_TM,),
        in_specs=[
            pl.BlockSpec((MERGE_TM, DIFF_WIDTH), lambda i: (i, 0)),
            pl.BlockSpec((MERGE_TM, NA_WIDTH), lambda i: (i, 0)),
            pl.BlockSpec((MERGE_TM, D_MODEL), lambda i: (i, gate_blk)),
            pl.BlockSpec((MERGE_TM, D_MODEL), lambda i: (i, gate_blk + 1)),
            pl.BlockSpec((MERGE_TM, D_MODEL), lambda i: (i, 0)),
            const((DIFF_WIDTH, D_MODEL)),
            const((NA_WIDTH, D_MODEL)),
            const((D_MODEL, D_MODEL)),
            const((1, D_MODEL)),
            const((1, D_MODEL)),
        ],
        out_specs=(pl.BlockSpec((MERGE_TM, D_MODEL), lambda i: (i, 0)),
                   pl.BlockSpec((MERGE_TM, D_MODEL), lambda i: (i, 0))),
        compiler_params=_cparams(1, 56),
        name="merge",
    )(oa2d, ob2d, z2d, z2d, x2d, wa, wb, wo, ln_g, ln_b)


def _seq_conv(hmat, cw_ref, cb_ref):
    n, t = hmat.shape[0], F32_SUBLANES
    edge_row = lax.broadcasted_iota(jnp.int32, (t, hmat.shape[1]), 0)
    prev = pltpu.roll(hmat, 1, 0)
    nxt = pltpu.roll(hmat, n - 1, 0)
    prev = jnp.concatenate([jnp.where(edge_row == 0, 0.0, prev[:t]), prev[t:]], axis=0)
    nxt = jnp.concatenate([nxt[:n - t], jnp.where(edge_row == t - 1, 0.0, nxt[n - t:])], axis=0)
    return cb_ref[...] + prev * cw_ref[0:1, :] + hmat * cw_ref[1:2, :] + nxt * cw_ref[2:3, :]


def _row_chunked_dot(x_ref, w_ref):
    return jnp.concatenate(
        [jnp.dot(x_ref[pl.ds(r0, FFN_TM), :], w_ref[...], preferred_element_type=F32)
         for r0 in range(0, x_ref.shape[0], FFN_TM)], axis=0)


def _ffn_up_kernel(x_ref, wg_ref, wv_ref, cwg_ref, cwv_ref, cbg_ref, cbv_ref, o_ref):
    gate = _seq_conv(_row_chunked_dot(x_ref, wg_ref), cwg_ref, cbg_ref)
    val = _seq_conv(_row_chunked_dot(x_ref, wv_ref), cwv_ref, cbv_ref)
    gelu = 0.5 * gate * (1.0 + lax.erf(gate * (2.0 ** -0.5)))
    o_ref[...] = (gelu * val).astype(BF16)


def _ffn_up(x1b, w_up, conv_w, conv_b):
    m = x1b.shape[0]
    return pl.pallas_call(
        _ffn_up_kernel,
        out_shape=jax.ShapeDtypeStruct((m, D_FF), BF16),
        grid=(m // SEQ, FFN_NJ),
        in_specs=[
            pl.BlockSpec((SEQ, D_MODEL), lambda b, j: (b, 0)),
            pl.BlockSpec((D_MODEL, FFN_TN), lambda b, j: (0, j)),
            pl.BlockSpec((D_MODEL, FFN_TN), lambda b, j: (0, FFN_NJ + j)),
            pl.BlockSpec((3, FFN_TN), lambda b, j: (0, j)),
            pl.BlockSpec((3, FFN_TN), lambda b, j: (0, FFN_NJ + j)),
            pl.BlockSpec((1, FFN_TN), lambda b, j: (0, j)),
            pl.BlockSpec((1, FFN_TN), lambda b, j: (0, FFN_NJ + j)),
        ],
        out_specs=pl.BlockSpec((SEQ, FFN_TN), lambda b, j: (b, j)),
        compiler_params=_cparams(2, 56),
        name="ffn_up",
    )(x1b, w_up, w_up, conv_w, conv_w, conv_b, conv_b)


def _ffn_down_kernel(g_ref, w_ref, x1_ref, lg_ref, lb_ref, o_ref, acc_ref):
    kk = pl.program_id(1)

    @pl.when(kk == 0)
    def _():
        acc_ref[...] = DEEPNORM_ALPHA * x1_ref[...]

    acc_ref[...] += jnp.dot(g_ref[...], w_ref[...], preferred_element_type=F32)

    @pl.when(kk == pl.num_programs(1) - 1)
    def _():
        o_ref[...] = _layer_norm(acc_ref[...], lg_ref[...], lb_ref[...])


def _ffn_down(g2d, w_down, x1, ln_g, ln_b):
    m = g2d.shape[0]
    return pl.pallas_call(
        _ffn_down_kernel,
        out_shape=jax.ShapeDtypeStruct((m, D_MODEL), F32),
        grid=(m // DOWN_TM, D_FF // DOWN_TK),
        in_specs=[
            pl.BlockSpec((DOWN_TM, DOWN_TK), lambda i, k: (i, k)),
            pl.BlockSpec((DOWN_TK, D_MODEL), lambda i, k: (k, 0)),
            pl.BlockSpec((DOWN_TM, D_MODEL), lambda i, k: (i, 0)),
            pl.BlockSpec((1, D_MODEL), lambda i, k: (0, 0)),
            pl.BlockSpec((1, D_MODEL), lambda i, k: (0, 0)),
        ],
        out_specs=pl.BlockSpec((DOWN_TM, D_MODEL), lambda i, k: (i, 0)),
        scratch_shapes=[pltpu.VMEM((DOWN_TM, D_MODEL), F32)],
        compiler_params=_cparams(2, 56),
        name="ffn_down",
    )(g2d, w_down, x1, ln_g, ln_b)


def _z_col_scale():
    s = np.ones((1, IN_COLS), np.float32)
    s[0, :DIFF_WIDTH] = DIFF_QK_DIM ** -0.5 * LOG2E
    s[0, 3 * DIFF_WIDTH:3 * DIFF_WIDTH + NA_WIDTH] = NA_HEAD_DIM ** -0.5 * LOG2E
    return s


def _alibi_slopes():
    n = N_DIFF_HEADS
    return np.array([2.0 ** (-8.0 * (i + 1) / n) for i in range(n)], dtype=np.float32)


def kernel(x, w_in, b_in, lam_q1, lam_k1, lam_q2, lam_k2, subln_g, rpb, w_branch_a, w_branch_b, w_out,
           ln1_g, ln1_b, w_up, conv_w, conv_b, w_down, ln2_g, ln2_b):
    bsz, seq, d = x.shape
    assert (seq, d) == (SEQ, D_MODEL) and w_in.shape[0] == 1
    row = lambda a: a.reshape(1, -1).astype(F32)
    x2d = x.reshape(bsz * seq, d)

    z = _in_proj(x2d, w_in[0].astype(BF16), row(b_in[0]), jnp.asarray(_z_col_scale()))
    z3 = z.reshape(bsz, seq, IN_COLS)
    oa = _diff_attn(z3, jnp.asarray(_alibi_slopes()), row(lam_q1[0]), row(lam_k1[0]), row(lam_q2[0]),
                    row(lam_k2[0]), row(subln_g[0]))
    ob = _na_attn(z3, _na_bias_tables(rpb[0]))
    x1, x1b = _merge(oa.reshape(bsz * seq, DIFF_WIDTH), ob.reshape(bsz * seq, NA_WIDTH), z, x2d,
                     w_branch_a[0].astype(BF16), w_branch_b[0].astype(BF16), w_out[0].astype(BF16),
                     row(ln1_g[0]), row(ln1_b[0]))
    g = _ffn_up(x1b, w_up[0].astype(BF16), conv_w[0].astype(F32), row(conv_b[0]))
    out = _ffn_down(g, w_down[0].astype(BF16), x1, row(ln2_g[0]), row(ln2_b[0]))
    return out.reshape(bsz, seq, d)
```

```python
import math
from functools import partial

import jax
import jax.numpy as jnp
import numpy as np
from jax import lax
from jax.experimental import pallas as pl
from jax.experimental.pallas import tpu as pltpu

F32 = jnp.float32
BF16 = jnp.bfloat16

D_MODEL = 2048
SEQ = 2048
GRID_W = 64
GRID_ROWS = SEQ // GRID_W
N_DIFF_HEADS = 8
DIFF_QK_DIM = 64
DIFF_V_DIM = 2 * DIFF_QK_DIM
DIFF_WIDTH = N_DIFF_HEADS * DIFF_V_DIM
N_NA_HEADS = 8
NA_HEAD_DIM = 128
NA_WIDTH = N_NA_HEADS * NA_HEAD_DIM
NA_KR = 8
NA_KC = 16
IN_COLS = 3 * DIFF_WIDTH + 3 * NA_WIDTH + 2 * D_MODEL
D_FF = 5632
LN_EPS = 1e-5
RMS_EPS = 1e-5
DEEPNORM_ALPHA = 2.0 ** 0.25
LAMBDA_INIT = 0.8 - 0.6 * math.exp(-0.3 * 0)
LOG2E = 1.4426950408889634

HEAD_COLS = 128
QA_BLK, KA_BLK, VA_BLK = 0, 8, 16
QB_BLK, KB_BLK, VB_BLK = 24, 32, 40
GATE_COL0 = 3 * DIFF_WIDTH + 3 * NA_WIDTH

MIB = 1024 * 1024
INPROJ_TM, INPROJ_TN = 1024, 1024
DIFF_TQ = 1024
DIFF_SUB = 256
DIFF_ROWS = 16
NA_QROWS = 4
NA_KROWS = 12
NA_TQ = NA_QROWS * GRID_W
NA_TK = NA_KROWS * GRID_W
NA_GROUPS = GRID_ROWS // NA_QROWS
MERGE_TM = 512
FFN_TN = 512
FFN_NJ = D_FF // FFN_TN
FFN_TM = 512
F32_SUBLANES = 8
DOWN_TM, DOWN_TK = 1024, 512


def _cparams(n_axes, vmem_mib):
    return pltpu.CompilerParams(
        dimension_semantics=("arbitrary",) * n_axes,
        vmem_limit_bytes=vmem_mib * MIB,
    )


def _inproj_kernel(x_ref, w_ref, b_ref, s_ref, z_ref, xb_ref):
    j = pl.program_id(1)

    @pl.when(j == 0)
    def _():
        xb_ref[...] = x_ref[...].astype(BF16)

    acc = jnp.dot(xb_ref[...], w_ref[...], preferred_element_type=F32)
    acc = (acc + b_ref[...]) * s_ref[...]
    gate0 = GATE_COL0 // INPROJ_TN

    @pl.when(j < gate0)
    def _():
        z_ref[...] = acc.astype(BF16)

    @pl.when(j >= gate0)
    def _():
        z_ref[...] = (0.5 * jnp.tanh(0.5 * acc) + 0.5).astype(BF16)


def _in_proj(x2d, w_bf, b_row, scale_row):
    m = x2d.shape[0]
    return pl.pallas_call(
        _inproj_kernel,
        out_shape=jax.ShapeDtypeStruct((m, IN_COLS), BF16),
        grid=(m // INPROJ_TM, IN_COLS // INPROJ_TN),
        in_specs=[
            pl.BlockSpec((INPROJ_TM, D_MODEL), lambda i, j: (i, 0)),
            pl.BlockSpec((D_MODEL, INPROJ_TN), lambda i, j: (0, j)),
            pl.BlockSpec((1, INPROJ_TN), lambda i, j: (0, j)),
            pl.BlockSpec((1, INPROJ_TN), lambda i, j: (0, j)),
        ],
        out_specs=pl.BlockSpec((INPROJ_TM, INPROJ_TN), lambda i, j: (i, j)),
        scratch_shapes=[pltpu.VMEM((INPROJ_TM, D_MODEL), BF16)],
        compiler_params=_cparams(2, 48),
        name="in_proj",
    )(x2d, w_bf, b_row, scale_row)


def _diff_attn_kernel(slopes_ref, lq1_ref, lk1_ref, lq2_ref, lk2_ref, g_ref,
                      q_ref, k_ref, v_ref, o_ref, bias_ref, *sub_refs):
    n_sub = DIFF_TQ // DIFF_SUB
    s_refs, a_refs = sub_refs[:2 * n_sub], sub_refs[2 * n_sub:]
    h = pl.program_id(0)
    qi = pl.program_id(1)
    b = pl.program_id(2)

    @pl.when(b == 0)
    def _():
        rows = lax.broadcasted_iota(jnp.int32, (DIFF_TQ, SEQ), 0) + qi * DIFF_TQ
        cols = lax.broadcasted_iota(jnp.int32, (DIFF_TQ, SEQ), 1)
        dist = jnp.abs(rows - cols).astype(F32)
        bias_ref[...] = dist * (-LOG2E * slopes_ref[h])

    lam = (jnp.exp(jnp.sum(lq1_ref[...] * lk1_ref[...], axis=-1, keepdims=True))
           - jnp.exp(jnp.sum(lq2_ref[...] * lk2_ref[...], axis=-1, keepdims=True))
           + LAMBDA_INIT)

    k = k_ref[...]
    nt = (((1,), (1,)), ((), ()))
    subs = range(DIFF_TQ // DIFF_SUB)
    for t in subs:
        q = q_ref[pl.ds(t * DIFF_SUB, DIFF_SUB), :]
        lane = lax.broadcasted_iota(jnp.int32, q.shape, 1)
        zero = jnp.zeros_like(q)
        bias = bias_ref[pl.ds(t * DIFF_SUB, DIFF_SUB), :]
        s_refs[2 * t][...] = lax.dot_general(jnp.where(lane < DIFF_QK_DIM, q, zero), k, nt,
                                             preferred_element_type=F32) + bias
        s_refs[2 * t + 1][...] = lax.dot_general(jnp.where(lane >= DIFF_QK_DIM, q, zero), k, nt,
                                                 preferred_element_type=F32) + bias
    for t in subs:
        s1_ref, s2_ref, a_ref = s_refs[2 * t], s_refs[2 * t + 1], a_refs[t]
        inv_l1 = []
        for r0 in range(0, DIFF_SUB, DIFF_ROWS):
            halves = []
            for r in range(r0, r0 + DIFF_ROWS, F32_SUBLANES):
                rows = pl.ds(r, F32_SUBLANES)
                x1 = s1_ref[rows, :]
                p1 = jnp.exp2(x1 - jnp.max(x1, axis=-1, keepdims=True))
                l1 = jnp.sum(p1, axis=-1, keepdims=True)
                x2 = s2_ref[rows, :]
                p2 = jnp.exp2(x2 - jnp.max(x2, axis=-1, keepdims=True))
                l2 = jnp.sum(p2, axis=-1, keepdims=True)
                halves.append(p1 - p2 * (lam * l1 / l2))
                inv_l1.append(1.0 / l1)
            a_ref[pl.ds(r0, DIFF_ROWS), :] = jnp.concatenate(halves, axis=0).astype(BF16)
        o = jnp.dot(a_ref[...], v_ref[...], preferred_element_type=F32) * jnp.concatenate(inv_l1, axis=0)
        ms = jnp.mean(o * o, axis=-1, keepdims=True)
        o = o * lax.rsqrt(ms + RMS_EPS) * g_ref[...] * (1.0 - LAMBDA_INIT)
        o_ref[pl.ds(t * DIFF_SUB, DIFF_SUB), :] = o.astype(BF16)


def _diff_attn(z3, slopes, lq1, lk1, lq2, lk2, subln_g):
    bsz = z3.shape[0]
    vec = lambda n: pl.BlockSpec((1, n), lambda h, qi, b: (0, 0))
    return pl.pallas_call(
        _diff_attn_kernel,
        out_shape=jax.ShapeDtypeStruct((bsz, SEQ, DIFF_WIDTH), BF16),
        grid=(N_DIFF_HEADS, SEQ // DIFF_TQ, bsz),
        in_specs=[
            pl.BlockSpec(memory_space=pltpu.SMEM),
            vec(DIFF_QK_DIM), vec(DIFF_QK_DIM), vec(DIFF_QK_DIM), vec(DIFF_QK_DIM),
            vec(DIFF_V_DIM),
            pl.BlockSpec((None, DIFF_TQ, HEAD_COLS), lambda h, qi, b: (b, qi, QA_BLK + h)),
            pl.BlockSpec((None, SEQ, HEAD_COLS), lambda h, qi, b: (b, 0, KA_BLK + h)),
            pl.BlockSpec((None, SEQ, HEAD_COLS), lambda h, qi, b: (b, 0, VA_BLK + h)),
        ],
        out_specs=pl.BlockSpec((None, DIFF_TQ, HEAD_COLS), lambda h, qi, b: (b, qi, h)),
        scratch_shapes=([pltpu.VMEM((DIFF_TQ, SEQ), F32)]
                        + [pltpu.VMEM((DIFF_SUB, SEQ), F32)] * (2 * (DIFF_TQ // DIFF_SUB))
                        + [pltpu.VMEM((DIFF_SUB, SEQ), BF16)] * (DIFF_TQ // DIFF_SUB)),
        compiler_params=_cparams(3, 48),
        name="diff_attn",
    )(slopes, lq1, lk1, lq2, lk2, subln_g, z3, z3, z3)


def _na_key_row0(g):
    return min(max(g * NA_QROWS - NA_KR // 2, 0), GRID_ROWS - NA_KROWS)


def _na_table_id(g):
    return 0 if g == 0 else (2 if g == NA_GROUPS - 1 else 1)


def _na_bias_tables(rpb):
    n_dr, n_dc = 2 * NA_KR - 1, 2 * NA_KC - 1
    period = 2 * GRID_W
    w = jnp.pad(rpb.astype(F32), ((0, 0), (0, 0), (0, period - n_dc)))
    f = jnp.tile(w, (1, 1, GRID_W))[..., :GRID_W * (period - 1)]
    toe = f.reshape(N_NA_HEADS, n_dr, GRID_W, period - 1)[..., NA_KC - 1:NA_KC - 1 + GRID_W]
    toe = jnp.pad(toe.transpose(0, 2, 1, 3), ((0, 0), (0, 0), (NA_KROWS, NA_KROWS), (0, 0)))
    tabs = []
    for g in (0, 1, NA_GROUPS - 1):
        strips = []
        for rq in range(NA_QROWS):
            off = _na_key_row0(g) - (g * NA_QROWS + rq) + (NA_KR - 1) + NA_KROWS
            strips.append(toe[:, :, off:off + NA_KROWS, :].reshape(N_NA_HEADS, GRID_W, NA_TK))
        vals = jnp.stack(strips, axis=1).reshape(N_NA_HEADS, NA_TQ, NA_TK) * LOG2E
        qi, kj = np.arange(NA_TQ), np.arange(NA_TK)
        r, c = g * NA_QROWS + qi // GRID_W, qi % GRID_W
        kr, kc = _na_key_row0(g) + kj // GRID_W, kj % GRID_W
        row_start = np.clip(r - NA_KR // 2, 0, GRID_ROWS - NA_KR)
        col_start = np.clip(c - NA_KC // 2, 0, GRID_W - NA_KC)
        valid = ((kr[None, :] >= row_start[:, None]) & (kr[None, :] < row_start[:, None] + NA_KR)
                 & (kc[None, :] >= col_start[:, None]) & (kc[None, :] < col_start[:, None] + NA_KC))
        tabs.append(jnp.where(jnp.asarray(valid)[None], vals, -jnp.inf))
    return jnp.stack(tabs, axis=1)


def _na_kernel(bias_ref, q_ref, k_ref, v_ref, o_ref):
    nt = (((1,), (1,)), ((), ()))
    for g in range(NA_GROUPS):
        rows = pl.ds(g * NA_TQ, NA_TQ)
        keys = pl.ds(_na_key_row0(g) * GRID_W, NA_TK)
        s = lax.dot_general(q_ref[rows, :], k_ref[keys, :], nt, preferred_element_type=F32)
        s = s + bias_ref[_na_table_id(g)]
        p = jnp.exp2(s - jnp.max(s, axis=-1, keepdims=True))
        r = 1.0 / jnp.sum(p, axis=-1, keepdims=True)
        o = jnp.dot(p.astype(BF16), v_ref[keys, :], preferred_element_type=F32) * r
        o_ref[rows, :] = o.astype(BF16)


def _na_attn(z3, bias_tabs):
    bsz = z3.shape[0]
    head_block = lambda blk: pl.BlockSpec((None, SEQ, HEAD_COLS), lambda h, b: (b, 0, blk + h))
    return pl.pallas_call(
        _na_kernel,
        out_shape=jax.ShapeDtypeStruct((bsz, SEQ, NA_WIDTH), BF16),
        grid=(N_NA_HEADS, bsz),
        in_specs=[
            pl.BlockSpec((None, 3, NA_TQ, NA_TK), lambda h, b: (h, 0, 0, 0)),
            head_block(QB_BLK), head_block(KB_BLK), head_block(VB_BLK),
        ],
        out_specs=head_block(0),
        compiler_params=_cparams(2, 32),
        name="na_attn",
    )(bias_tabs, z3, z3, z3)


def _layer_norm(r, g, b):
    mu = jnp.mean(r, axis=-1, keepdims=True)
    d = r - mu
    var = jnp.mean(d * d, axis=-1, keepdims=True)
    return d * lax.rsqrt(var + LN_EPS) * g + b


def _merge_kernel(oa_ref, ob_ref, ga_ref, gb_ref, x_ref, wa_ref, wb_ref, wo_ref, g_ref, b_ref,
                  x1_ref, x1b_ref):
    ya = jnp.dot(oa_ref[...], wa_ref[...], preferred_element_type=F32)
    yb = jnp.dot(ob_ref[...], wb_ref[...], preferred_element_type=F32)
    y = ga_ref[...].astype(F32) * ya + gb_ref[...].astype(F32) * yb
    hproj = jnp.dot(y.astype(BF16), wo_ref[...], preferred_element_type=F32)
    x1 = _layer_norm(DEEPNORM_ALPHA * x_ref[...] + hproj, g_ref[...], b_ref[...])
    x1_ref[...] = x1
    x1b_ref[...] = x1.astype(BF16)


def _merge(oa2d, ob2d, z2d, x2d, wa, wb, wo, ln_g, ln_b):
    m = x2d.shape[0]
    gate_blk = GATE_COL0 // D_MODEL
    const = lambda shape: pl.BlockSpec(shape, lambda i: (0, 0), pipeline_mode=pl.Buffered(1))
    return pl.pallas_call(
        _merge_kernel,
        out_shape=(jax.ShapeDtypeStruct((m, D_MODEL), F32), jax.ShapeDtypeStruct((m, D_MODEL), BF16)),
        grid=(m // MERGE_TM,),
        in_specs=[
            pl.BlockSpec((MERGE_TM, DIFF_WIDTH), lambda i: (i, 0)),
            pl.BlockSpec((MERGE_TM, NA_WIDTH), lambda i: (i, 0)),
            pl.BlockSpec((MERGE_TM, D_MODEL), lambda i: (i, gate_blk)),
            pl.BlockSpec((MERGE_TM, D_MODEL), lambda i: (i, gate_blk + 1)),
            pl.BlockSpec((MERGE_TM, D_MODEL), lambda i: (i, 0)),
            const((DIFF_WIDTH, D_MODEL)),
            const((NA_WIDTH, D_MODEL)),
            const((D_MODEL, D_MODEL)),
            const((1, D_MODEL)),
            const((1, D_MODEL)),
        ],
        out_specs=(pl.BlockSpec((MERGE_TM, D_MODEL), lambda i: (i, 0)),
                   pl.BlockSpec((MERGE_TM, D_MODEL), lambda i: (i, 0))),
        compiler_params=_cparams(1, 56),
        name="merge",
    )(oa2d, ob2d, z2d, z2d, x2d, wa, wb, wo, ln_g, ln_b)


def _seq_conv(hmat, cw_ref, cb_ref):
    n, t = hmat.shape[0], F32_SUBLANES
    edge_row = lax.broadcasted_iota(jnp.int32, (t, hmat.shape[1]), 0)
    prev = pltpu.roll(hmat, 1, 0)
    nxt = pltpu.roll(hmat, n - 1, 0)
    prev = jnp.concatenate([jnp.where(edge_row == 0, 0.0, prev[:t]), prev[t:]], axis=0)
    nxt = jnp.concatenate([nxt[:n - t], jnp.where(edge_row == t - 1, 0.0, nxt[n - t:])], axis=0)
    return cb_ref[...] + prev * cw_ref[0:1, :] + hmat * cw_ref[1:2, :] + nxt * cw_ref[2:3, :]


def _row_chunked_dot(x_ref, w_ref):
    return jnp.concatenate(
        [jnp.dot(x_ref[pl.ds(r0, FFN_TM), :], w_ref[...], preferred_element_type=F32)
         for r0 in range(0, x_ref.shape[0], FFN_TM)], axis=0)


def _ffn_up_kernel(x_ref, wg_ref, wv_ref, cwg_ref, cwv_ref, cbg_ref, cbv_ref, o_ref):
    gate = _seq_conv(_row_chunked_dot(x_ref, wg_ref), cwg_ref, cbg_ref)
    val = _seq_conv(_row_chunked_dot(x_ref, wv_ref), cwv_ref, cbv_ref)
    gelu = 0.5 * gate * (1.0 + lax.erf(gate * (2.0 ** -0.5)))
    o_ref[...] = (gelu * val).astype(BF16)


def _ffn_up(x1b, w_up, conv_w, conv_b):
    m = x1b.shape[0]
    return pl.pallas_call(
        _ffn_up_kernel,
        out_shape=jax.ShapeDtypeStruct((m, D_FF), BF16),
        grid=(m // SEQ, FFN_NJ),
        in_specs=[
            pl.BlockSpec((SEQ, D_MODEL), lambda b, j: (b, 0)),
            pl.BlockSpec((D_MODEL, FFN_TN), lambda b, j: (0, j)),
            pl.BlockSpec((D_MODEL, FFN_TN), lambda b, j: (0, FFN_NJ + j)),
            pl.BlockSpec((3, FFN_TN), lambda b, j: (0, j)),
            pl.BlockSpec((3, FFN_TN), lambda b, j: (0, FFN_NJ + j)),
            pl.BlockSpec((1, FFN_TN), lambda b, j: (0, j)),
            pl.BlockSpec((1, FFN_TN), lambda b, j: (0, FFN_NJ + j)),
        ],
        out_specs=pl.BlockSpec((SEQ, FFN_TN), lambda b, j: (b, j)),
        compiler_params=_cparams(2, 56),
        name="ffn_up",
    )(x1b, w_up, w_up, conv_w, conv_w, conv_b, conv_b)


def _ffn_down_kernel(g_ref, w_ref, x1_ref, lg_ref, lb_ref, o_ref, acc_ref):
    kk = pl.program_id(1)

    @pl.when(kk == 0)
    def _():
        acc_ref[...] = DEEPNORM_ALPHA * x1_ref[...]

    acc_ref[...] += jnp.dot(g_ref[...], w_ref[...], preferred_element_type=F32)

    @pl.when(kk == pl.num_programs(1) - 1)
    def _():
        o_ref[...] = _layer_norm(acc_ref[...], lg_ref[...], lb_ref[...])


def _ffn_down(g2d, w_down, x1, ln_g, ln_b):
    m = g2d.shape[0]
    return pl.pallas_call(
        _ffn_down_kernel,
        out_shape=jax.ShapeDtypeStruct((m, D_MODEL), F32),
        grid=(m // DOWN_TM, D_FF // DOWN_TK),
        in_specs=[
            pl.BlockSpec((DOWN_TM, DOWN_TK), lambda i, k: (i, k)),
            pl.BlockSpec((DOWN_TK, D_MODEL), lambda i, k: (k, 0)),
            pl.BlockSpec((DOWN_TM, D_MODEL), lambda i, k: (i, 0)),
            pl.BlockSpec((1, D_MODEL), lambda i, k: (0, 0)),
            pl.BlockSpec((1, D_MODEL), lambda i, k: (0, 0)),
        ],
        out_specs=pl.BlockSpec((DOWN_TM, D_MODEL), lambda i, k: (i, 0)),
        scratch_shapes=[pltpu.VMEM((DOWN_TM, D_MODEL), F32)],
        compiler_params=_cparams(2, 56),
        name="ffn_down",
    )(g2d, w_down, x1, ln_g, ln_b)


def _z_col_scale():
    s = np.ones((1, IN_COLS), np.float32)
    s[0, :DIFF_WIDTH] = DIFF_QK_DIM ** -0.5 * LOG2E
    s[0, 3 * DIFF_WIDTH:3 * DIFF_WIDTH + NA_WIDTH] = NA_HEAD_DIM ** -0.5 * LOG2E
    return s


def _alibi_slopes():
    n = N_DIFF_HEADS
    return np.array([2.0 ** (-8.0 * (i + 1) / n) for i in range(n)], dtype=np.float32)


def kernel(x, w_in, b_in, lam_q1, lam_k1, lam_q2, lam_k2, subln_g, rpb, w_branch_a, w_branch_b, w_out,
           ln1_g, ln1_b, w_up, conv_w, conv_b, w_down, ln2_g, ln2_b):
    bsz, seq, d = x.shape
    assert (seq, d) == (SEQ, D_MODEL) and w_in.shape[0] == 1
    row = lambda a: a.reshape(1, -1).astype(F32)
    x2d = x.reshape(bsz * seq, d)

    z = _in_proj(x2d, w_in[0].astype(BF16), row(b_in[0]), jnp.asarray(_z_col_scale()))
    z3 = z.reshape(bsz, seq, IN_COLS)
    oa = _diff_attn(z3, jnp.asarray(_alibi_slopes()), row(lam_q1[0]), row(lam_k1[0]), row(lam_q2[0]),
                    row(lam_k2[0]), row(subln_g[0]))
    ob = _na_attn(z3, _na_bias_tables(rpb[0]))
    x1, x1b = _merge(oa.reshape(bsz * seq, DIFF_WIDTH), ob.reshape(bsz * seq, NA_WIDTH), z, x2d,
                     w_branch_a[0].astype(BF16), w_branch_b[0].astype(BF16), w_out[0].astype(BF16),
                     row(ln1_g[0]), row(ln1_b[0]))
    g = _ffn_up(x1b, w_up[0].astype(BF16), conv_w[0].astype(F32), row(conv_b[0]))
    out = _ffn_down(g, w_down[0].astype(BF16), x1, row(ln2_g[0]), row(ln2_b[0]))
    return out.reshape(bsz, seq, d)
```

```python
import math
from functools import partial

import jax
import jax.numpy as jnp
import numpy as np
from jax import lax
from jax.experimental import pallas as pl
from jax.experimental.pallas import tpu as pltpu

F32 = jnp.float32
BF16 = jnp.bfloat16

D_MODEL = 2048
SEQ = 2048
GRID_W = 64
GRID_ROWS = SEQ // GRID_W
N_DIFF_HEADS = 8
DIFF_QK_DIM = 64
DIFF_V_DIM = 2 * DIFF_QK_DIM
DIFF_WIDTH = N_DIFF_HEADS * DIFF_V_DIM
N_NA_HEADS = 8
NA_HEAD_DIM = 128
NA_WIDTH = N_NA_HEADS * NA_HEAD_DIM
NA_KR = 8
NA_KC = 16
IN_COLS = 3 * DIFF_WIDTH + 3 * NA_WIDTH + 2 * D_MODEL
D_FF = 5632
LN_EPS = 1e-5
RMS_EPS = 1e-5
DEEPNORM_ALPHA = 2.0 ** 0.25
LAMBDA_INIT = 0.8 - 0.6 * math.exp(-0.3 * 0)
LOG2E = 1.4426950408889634

HEAD_COLS = 128
QA_BLK, KA_BLK, VA_BLK = 0, 8, 16
QB_BLK, KB_BLK, VB_BLK = 24, 32, 40
GATE_COL0 = 3 * DIFF_WIDTH + 3 * NA_WIDTH

MIB = 1024 * 1024
INPROJ_TM, INPROJ_TN = 1024, 1024
INPROJ_SUB_M = 512
DIFF_TQ = 1024
DIFF_SUB = 256
DIFF_ROWS = 16
NA_QROWS = 4
NA_KROWS = 12
NA_TQ = NA_QROWS * GRID_W
NA_TK = NA_KROWS * GRID_W
NA_GROUPS = GRID_ROWS // NA_QROWS
MERGE_TM = 512
FFN_TN = 512
FFN_NJ = D_FF // FFN_TN
FFN_TM = 512
F32_SUBLANES = 8
DOWN_TM, DOWN_TK = 1024, 512


def _cparams(n_axes, vmem_mib):
    return pltpu.CompilerParams(
        dimension_semantics=("arbitrary",) * n_axes,
        vmem_limit_bytes=vmem_mib * MIB,
    )


def _inproj_kernel(x_ref, w_ref, b_ref, s_ref, z_ref, xb_ref):
    j = pl.program_id(1)

    @pl.when(j == 0)
    def _():
        xb_ref[...] = x_ref[...].astype(BF16)

    def project(finish):
        halves = range(0, INPROJ_TM, INPROJ_SUB_M)
        accs = [jnp.dot(xb_ref[pl.ds(r0, INPROJ_SUB_M), :], w_ref[...], preferred_element_type=F32)
                for r0 in halves]
        for r0, acc in zip(halves, accs):
            z_ref[pl.ds(r0, INPROJ_SUB_M), :] = finish((acc + b_ref[...]) * s_ref[...]).astype(BF16)

    gate0 = GATE_COL0 // INPROJ_TN

    @pl.when(j < gate0)
    def _():
        project(lambda acc: acc)

    @pl.when(j >= gate0)
    def _():
        project(lambda acc: 0.5 * jnp.tanh(0.5 * acc) + 0.5)


def _in_proj(x2d, w_bf, b_row, scale_row):
    m = x2d.shape[0]
    return pl.pallas_call(
        _inproj_kernel,
        out_shape=jax.ShapeDtypeStruct((m, IN_COLS), BF16),
        grid=(m // INPROJ_TM, IN_COLS // INPROJ_TN),
        in_specs=[
            pl.BlockSpec((INPROJ_TM, D_MODEL), lambda i, j: (i, 0)),
            pl.BlockSpec((D_MODEL, INPROJ_TN), lambda i, j: (0, j)),
            pl.BlockSpec((1, INPROJ_TN), lambda i, j: (0, j)),
            pl.BlockSpec((1, INPROJ_TN), lambda i, j: (0, j)),
        ],
        out_specs=pl.BlockSpec((INPROJ_TM, INPROJ_TN), lambda i, j: (i, j)),
        scratch_shapes=[pltpu.VMEM((INPROJ_TM, D_MODEL), BF16)],
        compiler_params=_cparams(2, 48),
        name="in_proj",
    )(x2d, w_bf, b_row, scale_row)


def _diff_attn_kernel(slopes_ref, lq1_ref, lk1_ref, lq2_ref, lk2_ref, g_ref,
                      q_ref, k_ref, v_ref, o_ref, bias_ref, *sub_refs):
    n_sub = DIFF_TQ // DIFF_SUB
    s_refs, a_refs = sub_refs[:2 * n_sub], sub_refs[2 * n_sub:]
    h = pl.program_id(0)
    qi = pl.program_id(1)
    b = pl.program_id(2)

    @pl.when(b == 0)
    def _():
        rows = lax.broadcasted_iota(jnp.int32, (DIFF_TQ, SEQ), 0) + qi * DIFF_TQ
        cols = lax.broadcasted_iota(jnp.int32, (DIFF_TQ, SEQ), 1)
        dist = jnp.abs(rows - cols).astype(F32)
        bias_ref[...] = dist * (-LOG2E * slopes_ref[h])

    lam = (jnp.exp(jnp.sum(lq1_ref[...] * lk1_ref[...], axis=-1, keepdims=True))
           - jnp.exp(jnp.sum(lq2_ref[...] * lk2_ref[...], axis=-1, keepdims=True))
           + LAMBDA_INIT)

    k = k_ref[...]
    nt = (((1,), (1,)), ((), ()))
    subs = range(DIFF_TQ // DIFF_SUB)
    for t in subs:
        q = q_ref[pl.ds(t * DIFF_SUB, DIFF_SUB), :]
        lane = lax.broadcasted_iota(jnp.int32, q.shape, 1)
        zero = jnp.zeros_like(q)
        bias = bias_ref[pl.ds(t * DIFF_SUB, DIFF_SUB), :]
        s_refs[2 * t][...] = lax.dot_general(jnp.where(lane < DIFF_QK_DIM, q, zero), k, nt,
                                             preferred_element_type=F32) + bias
        s_refs[2 * t + 1][...] = lax.dot_general(jnp.where(lane >= DIFF_QK_DIM, q, zero), k, nt,
                                                 preferred_element_type=F32) + bias
    for t in subs:
        s1_ref, s2_ref, a_ref = s_refs[2 * t], s_refs[2 * t + 1], a_refs[t]
        inv_l1 = []
        for r0 in range(0, DIFF_SUB, DIFF_ROWS):
            halves = []
            for r in range(r0, r0 + DIFF_ROWS, F32_SUBLANES):
                rows = pl.ds(r, F32_SUBLANES)
                x1 = s1_ref[rows, :]
                p1 = jnp.exp2(x1 - jnp.max(x1, axis=-1, keepdims=True))
                l1 = jnp.sum(p1, axis=-1, keepdims=True)
                x2 = s2_ref[rows, :]
                p2 = jnp.exp2(x2 - jnp.max(x2, axis=-1, keepdims=True))
                l2 = jnp.sum(p2, axis=-1, keepdims=True)
                halves.append(p1 - p2 * (lam * l1 / l2))
                inv_l1.append(1.0 / l1)
            a_ref[pl.ds(r0, DIFF_ROWS), :] = jnp.concatenate(halves, axis=0).astype(BF16)
        o = jnp.dot(a_ref[...], v_ref[...], preferred_element_type=F32) * jnp.concatenate(inv_l1, axis=0)
        ms = jnp.mean(o * o, axis=-1, keepdims=True)
        o = o * lax.rsqrt(ms + RMS_EPS) * g_ref[...] * (1.0 - LAMBDA_INIT)
        o_ref[pl.ds(t * DIFF_SUB, DIFF_SUB), :] = o.astype(BF16)


def _diff_attn(z3, slopes, lq1, lk1, lq2, lk2, subln_g):
    bsz = z3.shape[0]
    vec = lambda n: pl.BlockSpec((1, n), lambda h, qi, b: (0, 0))
    return pl.pallas_call(
        _diff_attn_kernel,
        out_shape=jax.ShapeDtypeStruct((bsz, SEQ, DIFF_WIDTH), BF16),
        grid=(N_DIFF_HEADS, SEQ // DIFF_TQ, bsz),
        in_specs=[
            pl.BlockSpec(memory_space=pltpu.SMEM),
            vec(DIFF_QK_DIM), vec(DIFF_QK_DIM), vec(DIFF_QK_DIM), vec(DIFF_QK_DIM),
            vec(DIFF_V_DIM),
            pl.BlockSpec((None, DIFF_TQ, HEAD_COLS), lambda h, qi, b: (b, qi, QA_BLK + h)),
            pl.BlockSpec((None, SEQ, HEAD_COLS), lambda h, qi, b: (b, 0, KA_BLK + h)),
            pl.BlockSpec((None, SEQ, HEAD_COLS), lambda h, qi, b: (b, 0, VA_BLK + h)),
        ],
        out_specs=pl.BlockSpec((None, DIFF_TQ, HEAD_COLS), lambda h, qi, b: (b, qi, h)),
        scratch_shapes=([pltpu.VMEM((DIFF_TQ, SEQ), F32)]
                        + [pltpu.VMEM((DIFF_SUB, SEQ), F32)] * (2 * (DIFF_TQ // DIFF_SUB))
                        + [pltpu.VMEM((DIFF_SUB, SEQ), BF16)] * (DIFF_TQ // DIFF_SUB)),
        compiler_params=_cparams(3, 48),
        name="diff_attn",
    )(slopes, lq1, lk1, lq2, lk2, subln_g, z3, z3, z3)


def _na_key_row0(g):
    return min(max(g * NA_QROWS - NA_KR // 2, 0), GRID_ROWS - NA_KROWS)


def _na_table_id(g):
    return 0 if g == 0 else (2 if g == NA_GROUPS - 1 else 1)


def _na_bias_tables(rpb):
    n_dr, n_dc = 2 * NA_KR - 1, 2 * NA_KC - 1
    period = 2 * GRID_W
    w = jnp.pad(rpb.astype(F32), ((0, 0), (0, 0), (0, period - n_dc)))
    f = jnp.tile(w, (1, 1, GRID_W))[..., :GRID_W * (period - 1)]
    toe = f.reshape(N_NA_HEADS, n_dr, GRID_W, period - 1)[..., NA_KC - 1:NA_KC - 1 + GRID_W]
    toe = jnp.pad(toe.transpose(0, 2, 1, 3), ((0, 0), (0, 0), (NA_KROWS, NA_KROWS), (0, 0)))
    tabs = []
    for g in (0, 1, NA_GROUPS - 1):
        strips = []
        for rq in range(NA_QROWS):
            off = _na_key_row0(g) - (g * NA_QROWS + rq) + (NA_KR - 1) + NA_KROWS
            strips.append(toe[:, :, off:off + NA_KROWS, :].reshape(N_NA_HEADS, GRID_W, NA_TK))
        vals = jnp.stack(strips, axis=1).reshape(N_NA_HEADS, NA_TQ, NA_TK) * LOG2E
        qi, kj = np.arange(NA_TQ), np.arange(NA_TK)
        r, c = g * NA_QROWS + qi // GRID_W, qi % GRID_W
        kr, kc = _na_key_row0(g) + kj // GRID_W, kj % GRID_W
        row_start = np.clip(r - NA_KR // 2, 0, GRID_ROWS - NA_KR)
        col_start = np.clip(c - NA_KC // 2, 0, GRID_W - NA_KC)
        valid = ((kr[None, :] >= row_start[:, None]) & (kr[None, :] < row_start[:, None] + NA_KR)
                 & (kc[None, :] >= col_start[:, None]) & (kc[None, :] < col_start[:, None] + NA_KC))
        tabs.append(jnp.where(jnp.asarray(valid)[None], vals, -jnp.inf))
    return jnp.stack(tabs, axis=1)


def _na_kernel(bias_ref, q_ref, k_ref, v_ref, o_ref, *s_refs):
    nt = (((1,), (1,)), ((), ()))
    for g in range(NA_GROUPS):
        keys = pl.ds(_na_key_row0(g) * GRID_W, NA_TK)
        s = lax.dot_general(q_ref[pl.ds(g * NA_TQ, NA_TQ), :], k_ref[keys, :], nt, preferred_element_type=F32)
        s_refs[g][...] = s + bias_ref[_na_table_id(g)]
    for g in range(NA_GROUPS):
        rows = pl.ds(g * NA_TQ, NA_TQ)
        keys = pl.ds(_na_key_row0(g) * GRID_W, NA_TK)
        s = s_refs[g][...]
        p = jnp.exp2(s - jnp.max(s, axis=-1, keepdims=True))
        r = 1.0 / jnp.sum(p, axis=-1, keepdims=True)
        o = jnp.dot(p.astype(BF16), v_ref[keys, :], preferred_element_type=F32) * r
        o_ref[rows, :] = o.astype(BF16)


def _na_attn(z3, bias_tabs):
    bsz = z3.shape[0]
    head_block = lambda blk: pl.BlockSpec((None, SEQ, HEAD_COLS), lambda h, b: (b, 0, blk + h))
    return pl.pallas_call(
        _na_kernel,
        out_shape=jax.ShapeDtypeStruct((bsz, SEQ, NA_WIDTH), BF16),
        grid=(N_NA_HEADS, bsz),
        in_specs=[
            pl.BlockSpec((None, 3, NA_TQ, NA_TK), lambda h, b: (h, 0, 0, 0)),
            head_block(QB_BLK), head_block(KB_BLK), head_block(VB_BLK),
        ],
        out_specs=head_block(0),
        scratch_shapes=[pltpu.VMEM((NA_TQ, NA_TK), F32)] * NA_GROUPS,
        compiler_params=_cparams(2, 32),
        name="na_attn",
    )(bias_tabs, z3, z3, z3)


def _layer_norm(r, g, b):
    mu = jnp.mean(r, axis=-1, keepdims=True)
    d = r - mu
    var = jnp.mean(d * d, axis=-1, keepdims=True)
    return d * lax.rsqrt(var + LN_EPS) * g + b


def _merge_kernel(oa_ref, ob_ref, ga_ref, gb_ref, x_ref, wa_ref, wb_ref, wo_ref, g_ref, b_ref,
                  x1_ref, x1b_ref):
    ya = jnp.dot(oa_ref[...], wa_ref[...], preferred_element_type=F32)
    yb = jnp.dot(ob_ref[...], wb_ref[...], preferred_element_type=F32)
    y = ga_ref[...].astype(F32) * ya + gb_ref[...].astype(F32) * yb
    hproj = jnp.dot(y.astype(BF16), wo_ref[...], preferred_element_type=F32)
    x1 = _layer_norm(DEEPNORM_ALPHA * x_ref[...] + hproj, g_ref[...], b_ref[...])
    x1_ref[...] = x1
    x1b_ref[...] = x1.astype(BF16)


def _merge(oa2d, ob2d, z2d, x2d, wa, wb, wo, ln_g, ln_b):
    m = x2d.shape[0]
    gate_blk = GATE_COL0 // D_MODEL
    const = lambda shape: pl.BlockSpec(shape, lambda i: (0, 0), pipeline_mode=pl.Buffered(1))
    return pl.pallas_call(
        _merge_kernel,
        out_shape=(jax.ShapeDtypeStruct((m, D_MODEL), F32), jax.ShapeDtypeStruct((m, D_MODEL), BF16)),
        grid=(m // MERGE_TM,),
        in_specs=[
            pl.BlockSpec((MERGE_TM, DIFF_WIDTH), lambda i: (i, 0)),
            pl.BlockSpec((MERGE_TM, NA_WIDTH), lambda i: (i, 0)),
            pl.BlockSpec((MERGE_TM, D_MODEL), lambda i: (i, gate_blk)),
            pl.BlockSpec((MERGE_TM, D_MODEL), lambda i: (i, gate_blk + 1)),
            pl.BlockSpec((MERGE_TM, D_MODEL), lambda i: (i, 0)),
            const((DIFF_WIDTH, D_MODEL)),
            const((NA_WIDTH, D_MODEL)),
            const((D_MODEL, D_MODEL)),
            const((1, D_MODEL)),
            const((1, D_MODEL)),
        ],
        out_specs=(pl.BlockSpec((MERGE_TM, D_MODEL), lambda i: (i, 0)),
                   pl.BlockSpec((MERGE_TM, D_MODEL), lambda i: (i, 0))),
        compiler_params=_cparams(1, 56),
        name="merge",
    )(oa2d, ob2d, z2d, z2d, x2d, wa, wb, wo, ln_g, ln_b)


def _seq_conv(hmat, cw_ref, cb_ref):
    n, t = hmat.shape[0], F32_SUBLANES
    edge_row = lax.broadcasted_iota(jnp.int32, (t, hmat.shape[1]), 0)
    prev = pltpu.roll(hmat, 1, 0)
    nxt = pltpu.roll(hmat, n - 1, 0)
    prev = jnp.concatenate([jnp.where(edge_row == 0, 0.0, prev[:t]), prev[t:]], axis=0)
    nxt = jnp.concatenate([nxt[:n - t], jnp.where(edge_row == t - 1, 0.0, nxt[n - t:])], axis=0)
    return cb_ref[...] + prev * cw_ref[0:1, :] + hmat * cw_ref[1:2, :] + nxt * cw_ref[2:3, :]


def _row_chunked_dot(x_ref, w_ref):
    return jnp.concatenate(
        [jnp.dot(x_ref[pl.ds(r0, FFN_TM), :], w_ref[...], preferred_element_type=F32)
         for r0 in range(0, x_ref.shape[0], FFN_TM)], axis=0)


def _ffn_up_kernel(x_ref, wg_ref, wv_ref, cwg_ref, cwv_ref, cbg_ref, cbv_ref, o_ref):
    gate = _seq_conv(_row_chunked_dot(x_ref, wg_ref), cwg_ref, cbg_ref)
    val = _seq_conv(_row_chunked_dot(x_ref, wv_ref), cwv_ref, cbv_ref)
    gelu = 0.5 * gate * (1.0 + lax.erf(gate * (2.0 ** -0.5)))
    o_ref[...] = (gelu * val).astype(BF16)


def _ffn_up(x1b, w_up, conv_w, conv_b):
    m = x1b.shape[0]
    return pl.pallas_call(
        _ffn_up_kernel,
        out_shape=jax.ShapeDtypeStruct((m, D_FF), BF16),
        grid=(m // SEQ, FFN_NJ),
        in_specs=[
            pl.BlockSpec((SEQ, D_MODEL), lambda b, j: (b, 0)),
            pl.BlockSpec((D_MODEL, FFN_TN), lambda b, j: (0, j)),
            pl.BlockSpec((D_MODEL, FFN_TN), lambda b, j: (0, FFN_NJ + j)),
            pl.BlockSpec((3, FFN_TN), lambda b, j: (0, j)),
            pl.BlockSpec((3, FFN_TN), lambda b, j: (0, FFN_NJ + j)),
            pl.BlockSpec((1, FFN_TN), lambda b, j: (0, j)),
            pl.BlockSpec((1, FFN_TN), lambda b, j: (0, FFN_NJ + j)),
        ],
        out_specs=pl.BlockSpec((SEQ, FFN_TN), lambda b, j: (b, j)),
        compiler_params=_cparams(2, 56),
        name="ffn_up",
    )(x1b, w_up, w_up, conv_w, conv_w, conv_b, conv_b)


def _ffn_down_kernel(g_ref, w_ref, x1_ref, lg_ref, lb_ref, o_ref, acc_ref):
    kk = pl.program_id(1)

    @pl.when(kk == 0)
    def _():
        acc_ref[...] = DEEPNORM_ALPHA * x1_ref[...]

    acc_ref[...] += jnp.dot(g_ref[...], w_ref[...], preferred_element_type=F32)

    @pl.when(kk == pl.num_programs(1) - 1)
    def _():
        o_ref[...] = _layer_norm(acc_ref[...], lg_ref[...], lb_ref[...])


def _ffn_down(g2d, w_down, x1, ln_g, ln_b):
    m = g2d.shape[0]
    return pl.pallas_call(
        _ffn_down_kernel,
        out_shape=jax.ShapeDtypeStruct((m, D_MODEL), F32),
        grid=(m // DOWN_TM, D_FF // DOWN_TK),
        in_specs=[
            pl.BlockSpec((DOWN_TM, DOWN_TK), lambda i, k: (i, k)),
            pl.BlockSpec((DOWN_TK, D_MODEL), lambda i, k: (k, 0)),
            pl.BlockSpec((DOWN_TM, D_MODEL), lambda i, k: (i, 0)),
            pl.BlockSpec((1, D_MODEL), lambda i, k: (0, 0)),
            pl.BlockSpec((1, D_MODEL), lambda i, k: (0, 0)),
        ],
        out_specs=pl.BlockSpec((DOWN_TM, D_MODEL), lambda i, k: (i, 0)),
        scratch_shapes=[pltpu.VMEM((DOWN_TM, D_MODEL), F32)],
        compiler_params=_cparams(2, 56),
        name="ffn_down",
    )(g2d, w_down, x1, ln_g, ln_b)


def _z_col_scale():
    s = np.ones((1, IN_COLS), np.float32)
    s[0, :DIFF_WIDTH] = DIFF_QK_DIM ** -0.5 * LOG2E
    s[0, 3 * DIFF_WIDTH:3 * DIFF_WIDTH + NA_WIDTH] = NA_HEAD_DIM ** -0.5 * LOG2E
    return s


def _alibi_slopes():
    n = N_DIFF_HEADS
    return np.array([2.0 ** (-8.0 * (i + 1) / n) for i in range(n)], dtype=np.float32)


def kernel(x, w_in, b_in, lam_q1, lam_k1, lam_q2, lam_k2, subln_g, rpb, w_branch_a, w_branch_b, w_out,
           ln1_g, ln1_b, w_up, conv_w, conv_b, w_down, ln2_g, ln2_b):
    bsz, seq, d = x.shape
    assert (seq, d) == (SEQ, D_MODEL) and w_in.shape[0] == 1
    row = lambda a: a.reshape(1, -1).astype(F32)
    x2d = x.reshape(bsz * seq, d)

    z = _in_proj(x2d, w_in[0].astype(BF16), row(b_in[0]), jnp.asarray(_z_col_scale()))
    z3 = z.reshape(bsz, seq, IN_COLS)
    oa = _diff_attn(z3, jnp.asarray(_alibi_slopes()), row(lam_q1[0]), row(lam_k1[0]), row(lam_q2[0]),
                    row(lam_k2[0]), row(subln_g[0]))
    ob = _na_attn(z3, _na_bias_tables(rpb[0]))
    x1, x1b = _merge(oa.reshape(bsz * seq, DIFF_WIDTH), ob.reshape(bsz * seq, NA_WIDTH), z, x2d,
                     w_branch_a[0].astype(BF16), w_branch_b[0].astype(BF16), w_out[0].astype(BF16),
                     row(ln1_g[0]), row(ln1_b[0]))
    g = _ffn_up(x1b, w_up[0].astype(BF16), conv_w[0].astype(F32), row(conv_b[0]))
    out = _ffn_down(g, w_down[0].astype(BF16), x1, row(ln2_g[0]), row(ln2_b[0]))
    return out.reshape(bsz, seq, d)
```

```python
import math
from functools import partial

import jax
import jax.numpy as jnp
import numpy as np
from jax import lax
from jax.experimental import pallas as pl
from jax.experimental.pallas import tpu as pltpu

F32 = jnp.float32
BF16 = jnp.bfloat16

D_MODEL = 2048
SEQ = 2048
GRID_W = 64
GRID_ROWS = SEQ // GRID_W
N_DIFF_HEADS = 8
DIFF_QK_DIM = 64
DIFF_V_DIM = 2 * DIFF_QK_DIM
DIFF_WIDTH = N_DIFF_HEADS * DIFF_V_DIM
N_NA_HEADS = 8
NA_HEAD_DIM = 128
NA_WIDTH = N_NA_HEADS * NA_HEAD_DIM
NA_KR = 8
NA_KC = 16
IN_COLS = 3 * DIFF_WIDTH + 3 * NA_WIDTH + 2 * D_MODEL
D_FF = 5632
LN_EPS = 1e-5
RMS_EPS = 1e-5
DEEPNORM_ALPHA = 2.0 ** 0.25
LAMBDA_INIT = 0.8 - 0.6 * math.exp(-0.3 * 0)
LOG2E = 1.4426950408889634

HEAD_COLS = 128
QA_BLK, KA_BLK, VA_BLK = 0, 8, 16
QB_BLK, KB_BLK, VB_BLK = 24, 32, 40
GATE_COL0 = 3 * DIFF_WIDTH + 3 * NA_WIDTH

MIB = 1024 * 1024
INPROJ_TM, INPROJ_TN = 1024, 1024
INPROJ_SUB_M = 512
DIFF_TQ = 1024
DIFF_SUB = 256
DIFF_ROWS = 16
NA_QROWS = 4
NA_KROWS = 12
NA_TQ = NA_QROWS * GRID_W
NA_TK = NA_KROWS * GRID_W
NA_GROUPS = GRID_ROWS // NA_QROWS
MERGE_TM = 512
FFN_TN = 512
FFN_NJ = D_FF // FFN_TN
FFN_TM = 512
F32_SUBLANES = 8
DOWN_TM = 512
DOWN_SUB_N = 512


def _cparams(n_axes, vmem_mib):
    return pltpu.CompilerParams(
        dimension_semantics=("arbitrary",) * n_axes,
        vmem_limit_bytes=vmem_mib * MIB,
    )


def _inproj_kernel(x_ref, w_ref, b_ref, s_ref, z_ref, xb_ref):
    j = pl.program_id(1)

    @pl.when(j == 0)
    def _():
        xb_ref[...] = x_ref[...].astype(BF16)

    def project(finish):
        halves = range(0, INPROJ_TM, INPROJ_SUB_M)
        accs = [jnp.dot(xb_ref[pl.ds(r0, INPROJ_SUB_M), :], w_ref[...], preferred_element_type=F32)
                for r0 in halves]
        for r0, acc in zip(halves, accs):
            z_ref[pl.ds(r0, INPROJ_SUB_M), :] = finish((acc + b_ref[...]) * s_ref[...]).astype(BF16)

    gate0 = GATE_COL0 // INPROJ_TN

    @pl.when(j < gate0)
    def _():
        project(lambda acc: acc)

    @pl.when(j >= gate0)
    def _():
        project(lambda acc: 0.5 * jnp.tanh(0.5 * acc) + 0.5)


def _in_proj(x2d, w_bf, b_row, scale_row):
    m = x2d.shape[0]
    return pl.pallas_call(
        _inproj_kernel,
        out_shape=jax.ShapeDtypeStruct((m, IN_COLS), BF16),
        grid=(m // INPROJ_TM, IN_COLS // INPROJ_TN),
        in_specs=[
            pl.BlockSpec((INPROJ_TM, D_MODEL), lambda i, j: (i, 0)),
            pl.BlockSpec((D_MODEL, INPROJ_TN), lambda i, j: (0, j)),
            pl.BlockSpec((1, INPROJ_TN), lambda i, j: (0, j)),
            pl.BlockSpec((1, INPROJ_TN), lambda i, j: (0, j)),
        ],
        out_specs=pl.BlockSpec((INPROJ_TM, INPROJ_TN), lambda i, j: (i, j)),
        scratch_shapes=[pltpu.VMEM((INPROJ_TM, D_MODEL), BF16)],
        compiler_params=_cparams(2, 48),
        name="in_proj",
    )(x2d, w_bf, b_row, scale_row)


def _diff_attn_kernel(slopes_ref, lq1_ref, lk1_ref, lq2_ref, lk2_ref, g_ref,
                      q_ref, k_ref, v_ref, o_ref, bias_ref, *sub_refs):
    n_sub = DIFF_TQ // DIFF_SUB
    s_refs, a_refs = sub_refs[:2 * n_sub], sub_refs[2 * n_sub:]
    h = pl.program_id(0)
    qi = pl.program_id(1)
    b = pl.program_id(2)

    @pl.when(b == 0)
    def _():
        rows = lax.broadcasted_iota(jnp.int32, (DIFF_TQ, SEQ), 0) + qi * DIFF_TQ
        cols = lax.broadcasted_iota(jnp.int32, (DIFF_TQ, SEQ), 1)
        dist = jnp.abs(rows - cols).astype(F32)
        bias_ref[...] = dist * (-LOG2E * slopes_ref[h])

    lam = (jnp.exp(jnp.sum(lq1_ref[...] * lk1_ref[...], axis=-1, keepdims=True))
           - jnp.exp(jnp.sum(lq2_ref[...] * lk2_ref[...], axis=-1, keepdims=True))
           + LAMBDA_INIT)

    k = k_ref[...]
    nt = (((1,), (1,)), ((), ()))
    subs = range(DIFF_TQ // DIFF_SUB)
    for t in subs:
        q = q_ref[pl.ds(t * DIFF_SUB, DIFF_SUB), :]
        lane = lax.broadcasted_iota(jnp.int32, q.shape, 1)
        zero = jnp.zeros_like(q)
        bias = bias_ref[pl.ds(t * DIFF_SUB, DIFF_SUB), :]
        s_refs[2 * t][...] = lax.dot_general(jnp.where(lane < DIFF_QK_DIM, q, zero), k, nt,
                                             preferred_element_type=F32) + bias
        s_refs[2 * t + 1][...] = lax.dot_general(jnp.where(lane >= DIFF_QK_DIM, q, zero), k, nt,
                                                 preferred_element_type=F32) + bias
    for t in subs:
        s1_ref, s2_ref, a_ref = s_refs[2 * t], s_refs[2 * t + 1], a_refs[t]
        inv_l1 = []
        for r0 in range(0, DIFF_SUB, DIFF_ROWS):
            halves = []
            for r in range(r0, r0 + DIFF_ROWS, F32_SUBLANES):
                rows = pl.ds(r, F32_SUBLANES)
                x1 = s1_ref[rows, :]
                p1 = jnp.exp2(x1 - jnp.max(x1, axis=-1, keepdims=True))
                l1 = jnp.sum(p1, axis=-1, keepdims=True)
                x2 = s2_ref[rows, :]
                p2 = jnp.exp2(x2 - jnp.max(x2, axis=-1, keepdims=True))
                l2 = jnp.sum(p2, axis=-1, keepdims=True)
                halves.append(p1 - p2 * (lam * l1 / l2))
                inv_l1.append(1.0 / l1)
            a_ref[pl.ds(r0, DIFF_ROWS), :] = jnp.concatenate(halves, axis=0).astype(BF16)
        o = jnp.dot(a_ref[...], v_ref[...], preferred_element_type=F32) * jnp.concatenate(inv_l1, axis=0)
        ms = jnp.mean(o * o, axis=-1, keepdims=True)
        o = o * lax.rsqrt(ms + RMS_EPS) * g_ref[...] * (1.0 - LAMBDA_INIT)
        o_ref[pl.ds(t * DIFF_SUB, DIFF_SUB), :] = o.astype(BF16)


def _diff_attn(z3, slopes, lq1, lk1, lq2, lk2, subln_g):
    bsz = z3.shape[0]
    vec = lambda n: pl.BlockSpec((1, n), lambda h, qi, b: (0, 0))
    return pl.pallas_call(
        _diff_attn_kernel,
        out_shape=jax.ShapeDtypeStruct((bsz, SEQ, DIFF_WIDTH), BF16),
        grid=(N_DIFF_HEADS, SEQ // DIFF_TQ, bsz),
        in_specs=[
            pl.BlockSpec(memory_space=pltpu.SMEM),
            vec(DIFF_QK_DIM), vec(DIFF_QK_DIM), vec(DIFF_QK_DIM), vec(DIFF_QK_DIM),
            vec(DIFF_V_DIM),
            pl.BlockSpec((None, DIFF_TQ, HEAD_COLS), lambda h, qi, b: (b, qi, QA_BLK + h)),
            pl.BlockSpec((None, SEQ, HEAD_COLS), lambda h, qi, b: (b, 0, KA_BLK + h)),
            pl.BlockSpec((None, SEQ, HEAD_COLS), lambda h, qi, b: (b, 0, VA_BLK + h)),
        ],
        out_specs=pl.BlockSpec((None, DIFF_TQ, HEAD_COLS), lambda h, qi, b: (b, qi, h)),
        scratch_shapes=([pltpu.VMEM((DIFF_TQ, SEQ), F32)]
                        + [pltpu.VMEM((DIFF_SUB, SEQ), F32)] * (2 * (DIFF_TQ // DIFF_SUB))
                        + [pltpu.VMEM((DIFF_SUB, SEQ), BF16)] * (DIFF_TQ // DIFF_SUB)),
        compiler_params=_cparams(3, 48),
        name="diff_attn",
    )(slopes, lq1, lk1, lq2, lk2, subln_g, z3, z3, z3)


def _na_key_row0(g):
    return min(max(g * NA_QROWS - NA_KR // 2, 0), GRID_ROWS - NA_KROWS)


def _na_table_id(g):
    return 0 if g == 0 else (2 if g == NA_GROUPS - 1 else 1)


def _na_bias_tables(rpb):
    n_dr, n_dc = 2 * NA_KR - 1, 2 * NA_KC - 1
    period = 2 * GRID_W
    w = jnp.pad(rpb.astype(F32), ((0, 0), (0, 0), (0, period - n_dc)))
    f = jnp.tile(w, (1, 1, GRID_W))[..., :GRID_W * (period - 1)]
    toe = f.reshape(N_NA_HEADS, n_dr, GRID_W, period - 1)[..., NA_KC - 1:NA_KC - 1 + GRID_W]
    toe = jnp.pad(toe.transpose(0, 2, 1, 3), ((0, 0), (0, 0), (NA_KROWS, NA_KROWS), (0, 0)))
    tabs = []
    for g in (0, 1, NA_GROUPS - 1):
        strips = []
        for rq in range(NA_QROWS):
            off = _na_key_row0(g) - (g * NA_QROWS + rq) + (NA_KR - 1) + NA_KROWS
            strips.append(toe[:, :, off:off + NA_KROWS, :].reshape(N_NA_HEADS, GRID_W, NA_TK))
        vals = jnp.stack(strips, axis=1).reshape(N_NA_HEADS, NA_TQ, NA_TK) * LOG2E
        qi, kj = np.arange(NA_TQ), np.arange(NA_TK)
        r, c = g * NA_QROWS + qi // GRID_W, qi % GRID_W
        kr, kc = _na_key_row0(g) + kj // GRID_W, kj % GRID_W
        row_start = np.clip(r - NA_KR // 2, 0, GRID_ROWS - NA_KR)
        col_start = np.clip(c - NA_KC // 2, 0, GRID_W - NA_KC)
        valid = ((kr[None, :] >= row_start[:, None]) & (kr[None, :] < row_start[:, None] + NA_KR)
                 & (kc[None, :] >= col_start[:, None]) & (kc[None, :] < col_start[:, None] + NA_KC))
        tabs.append(jnp.where(jnp.asarray(valid)[None], vals, -jnp.inf))
    return jnp.stack(tabs, axis=1)


def _na_kernel(bias_ref, q_ref, k_ref, v_ref, o_ref, *s_refs):
    nt = (((1,), (1,)), ((), ()))
    for g in range(NA_GROUPS):
        keys = pl.ds(_na_key_row0(g) * GRID_W, NA_TK)
        s = lax.dot_general(q_ref[pl.ds(g * NA_TQ, NA_TQ), :], k_ref[keys, :], nt, preferred_element_type=F32)
        s_refs[g][...] = s + bias_ref[_na_table_id(g)]
    for g in range(NA_GROUPS):
        rows = pl.ds(g * NA_TQ, NA_TQ)
        keys = pl.ds(_na_key_row0(g) * GRID_W, NA_TK)
        s = s_refs[g][...]
        p = jnp.exp2(s - jnp.max(s, axis=-1, keepdims=True))
        r = 1.0 / jnp.sum(p, axis=-1, keepdims=True)
        o = jnp.dot(p.astype(BF16), v_ref[keys, :], preferred_element_type=F32) * r
        o_ref[rows, :] = o.astype(BF16)


def _na_attn(z3, bias_tabs):
    bsz = z3.shape[0]
    head_block = lambda blk: pl.BlockSpec((None, SEQ, HEAD_COLS), lambda h, b: (b, 0, blk + h))
    return pl.pallas_call(
        _na_kernel,
        out_shape=jax.ShapeDtypeStruct((bsz, SEQ, NA_WIDTH), BF16),
        grid=(N_NA_HEADS, bsz),
        in_specs=[
            pl.BlockSpec((None, 3, NA_TQ, NA_TK), lambda h, b: (h, 0, 0, 0)),
            head_block(QB_BLK), head_block(KB_BLK), head_block(VB_BLK),
        ],
        out_specs=head_block(0),
        scratch_shapes=[pltpu.VMEM((NA_TQ, NA_TK), F32)] * NA_GROUPS,
        compiler_params=_cparams(2, 32),
        name="na_attn",
    )(bias_tabs, z3, z3, z3)


def _layer_norm(r, g, b):
    mu = jnp.mean(r, axis=-1, keepdims=True)
    d = r - mu
    var = jnp.mean(d * d, axis=-1, keepdims=True)
    return d * lax.rsqrt(var + LN_EPS) * g + b


def _merge_kernel(oa_ref, ob_ref, ga_ref, gb_ref, x_ref, wa_ref, wb_ref, wo_ref, g_ref, b_ref,
                  x1_ref, x1b_ref):
    ya = jnp.dot(oa_ref[...], wa_ref[...], preferred_element_type=F32)
    yb = jnp.dot(ob_ref[...], wb_ref[...], preferred_element_type=F32)
    y = ga_ref[...].astype(F32) * ya + gb_ref[...].astype(F32) * yb
    hproj = jnp.dot(y.astype(BF16), wo_ref[...], preferred_element_type=F32)
    x1 = _layer_norm(DEEPNORM_ALPHA * x_ref[...] + hproj, g_ref[...], b_ref[...])
    x1_ref[...] = x1
    x1b_ref[...] = x1.astype(BF16)


def _merge(oa2d, ob2d, z2d, x2d, wa, wb, wo, ln_g, ln_b):
    m = x2d.shape[0]
    gate_blk = GATE_COL0 // D_MODEL
    const = lambda shape: pl.BlockSpec(shape, lambda i: (0, 0), pipeline_mode=pl.Buffered(1))
    return pl.pallas_call(
        _merge_kernel,
        out_shape=(jax.ShapeDtypeStruct((m, D_MODEL), F32), jax.ShapeDtypeStruct((m, D_MODEL), BF16)),
        grid=(m // MERGE_TM,),
        in_specs=[
            pl.BlockSpec((MERGE_TM, DIFF_WIDTH), lambda i: (i, 0)),
            pl.BlockSpec((MERGE_TM, NA_WIDTH), lambda i: (i, 0)),
            pl.BlockSpec((MERGE_TM, D_MODEL), lambda i: (i, gate_blk)),
            pl.BlockSpec((MERGE_TM, D_MODEL), lambda i: (i, gate_blk + 1)),
            pl.BlockSpec((MERGE_TM, D_MODEL), lambda i: (i, 0)),
            const((DIFF_WIDTH, D_MODEL)),
            const((NA_WIDTH, D_MODEL)),
            const((D_MODEL, D_MODEL)),
            const((1, D_MODEL)),
            const((1, D_MODEL)),
        ],
        out_specs=(pl.BlockSpec((MERGE_TM, D_MODEL), lambda i: (i, 0)),
                   pl.BlockSpec((MERGE_TM, D_MODEL), lambda i: (i, 0))),
        compiler_params=_cparams(1, 56),
        name="merge",
    )(oa2d, ob2d, z2d, z2d, x2d, wa, wb, wo, ln_g, ln_b)


def _seq_conv(hmat, cw_ref, cb_ref):
    n, t = hmat.shape[0], F32_SUBLANES
    edge_row = lax.broadcasted_iota(jnp.int32, (t, hmat.shape[1]), 0)
    prev = pltpu.roll(hmat, 1, 0)
    nxt = pltpu.roll(hmat, n - 1, 0)
    prev = jnp.concatenate([jnp.where(edge_row == 0, 0.0, prev[:t]), prev[t:]], axis=0)
    nxt = jnp.concatenate([nxt[:n - t], jnp.where(edge_row == t - 1, 0.0, nxt[n - t:])], axis=0)
    return cb_ref[...] + prev * cw_ref[0:1, :] + hmat * cw_ref[1:2, :] + nxt * cw_ref[2:3, :]


def _row_chunked_dot(x_ref, w_ref):
    return jnp.concatenate(
        [jnp.dot(x_ref[pl.ds(r0, FFN_TM), :], w_ref[...], preferred_element_type=F32)
         for r0 in range(0, x_ref.shape[0], FFN_TM)], axis=0)


def _ffn_up_kernel(x_ref, wg_ref, wv_ref, cwg_ref, cwv_ref, cbg_ref, cbv_ref, o_ref):
    gate = _seq_conv(_row_chunked_dot(x_ref, wg_ref), cwg_ref, cbg_ref)
    val = _seq_conv(_row_chunked_dot(x_ref, wv_ref), cwv_ref, cbv_ref)
    gelu = 0.5 * gate * (1.0 + lax.erf(gate * (2.0 ** -0.5)))
    o_ref[...] = (gelu * val).astype(BF16)


def _ffn_up(x1b, w_up, conv_w, conv_b):
    m = x1b.shape[0]
    return pl.pallas_call(
        _ffn_up_kernel,
        out_shape=jax.ShapeDtypeStruct((m, D_FF), BF16),
        grid=(m // SEQ, FFN_NJ),
        in_specs=[
            pl.BlockSpec((SEQ, D_MODEL), lambda b, j: (b, 0)),
            pl.BlockSpec((D_MODEL, FFN_TN), lambda b, j: (0, j)),
            pl.BlockSpec((D_MODEL, FFN_TN), lambda b, j: (0, FFN_NJ + j)),
            pl.BlockSpec((3, FFN_TN), lambda b, j: (0, j)),
            pl.BlockSpec((3, FFN_TN), lambda b, j: (0, FFN_NJ + j)),
            pl.BlockSpec((1, FFN_TN), lambda b, j: (0, j)),
            pl.BlockSpec((1, FFN_TN), lambda b, j: (0, FFN_NJ + j)),
        ],
        out_specs=pl.BlockSpec((SEQ, FFN_TN), lambda b, j: (b, j)),
        compiler_params=_cparams(2, 56),
        name="ffn_up",
    )(x1b, w_up, w_up, conv_w, conv_w, conv_b, conv_b)


def _ffn_down_kernel(g_ref, w_ref, x1_ref, lg_ref, lb_ref, o_ref):
    g = g_ref[...]
    f = jnp.concatenate(
        [jnp.dot(g, w_ref[:, pl.ds(c0, DOWN_SUB_N)], preferred_element_type=F32)
         for c0 in range(0, D_MODEL, DOWN_SUB_N)], axis=1)
    o_ref[...] = _layer_norm(DEEPNORM_ALPHA * x1_ref[...] + f, lg_ref[...], lb_ref[...])


def _ffn_down(g2d, w_down, x1, ln_g, ln_b):
    m = g2d.shape[0]
    const = lambda shape: pl.BlockSpec(shape, lambda i: (0, 0), pipeline_mode=pl.Buffered(1))
    return pl.pallas_call(
        _ffn_down_kernel,
        out_shape=jax.ShapeDtypeStruct((m, D_MODEL), F32),
        grid=(m // DOWN_TM,),
        in_specs=[
            pl.BlockSpec((DOWN_TM, D_FF), lambda i: (i, 0)),
            const((D_FF, D_MODEL)),
            pl.BlockSpec((DOWN_TM, D_MODEL), lambda i: (i, 0)),
            const((1, D_MODEL)),
            const((1, D_MODEL)),
        ],
        out_specs=pl.BlockSpec((DOWN_TM, D_MODEL), lambda i: (i, 0)),
        compiler_params=_cparams(1, 60),
        name="ffn_down",
    )(g2d, w_down, x1, ln_g, ln_b)


def _z_col_scale():
    s = np.ones((1, IN_COLS), np.float32)
    s[0, :DIFF_WIDTH] = DIFF_QK_DIM ** -0.5 * LOG2E
    s[0, 3 * DIFF_WIDTH:3 * DIFF_WIDTH + NA_WIDTH] = NA_HEAD_DIM ** -0.5 * LOG2E
    return s


def _alibi_slopes():
    n = N_DIFF_HEADS
    return np.array([2.0 ** (-8.0 * (i + 1) / n) for i in range(n)], dtype=np.float32)


def kernel(x, w_in, b_in, lam_q1, lam_k1, lam_q2, lam_k2, subln_g, rpb, w_branch_a, w_branch_b, w_out,
           ln1_g, ln1_b, w_up, conv_w, conv_b, w_down, ln2_g, ln2_b):
    bsz, seq, d = x.shape
    assert (seq, d) == (SEQ, D_MODEL) and w_in.shape[0] == 1
    row = lambda a: a.reshape(1, -1).astype(F32)
    x2d = x.reshape(bsz * seq, d)

    z = _in_proj(x2d, w_in[0].astype(BF16), row(b_in[0]), jnp.asarray(_z_col_scale()))
    z3 = z.reshape(bsz, seq, IN_COLS)
    oa = _diff_attn(z3, jnp.asarray(_alibi_slopes()), row(lam_q1[0]), row(lam_k1[0]), row(lam_q2[0]),
                    row(lam_k2[0]), row(subln_g[0]))
    ob = _na_attn(z3, _na_bias_tables(rpb[0]))
    x1, x1b = _merge(oa.reshape(bsz * seq, DIFF_WIDTH), ob.reshape(bsz * seq, NA_WIDTH), z, x2d,
                     w_branch_a[0].astype(BF16), w_branch_b[0].astype(BF16), w_out[0].astype(BF16),
                     row(ln1_g[0]), row(ln1_b[0]))
    g = _ffn_up(x1b, w_up[0].astype(BF16), conv_w[0].astype(F32), row(conv_b[0]))
    out = _ffn_down(g, w_down[0].astype(BF16), x1, row(ln2_g[0]), row(ln2_b[0]))
    return out.reshape(bsz, seq, d)
```

```python
import math
from functools import partial

import jax
import jax.numpy as jnp
import numpy as np
from jax import lax
from jax.experimental import pallas as pl
from jax.experimental.pallas import tpu as pltpu

F32 = jnp.float32
BF16 = jnp.bfloat16

D_MODEL = 2048
SEQ = 2048
GRID_W = 64
GRID_ROWS = SEQ // GRID_W
N_DIFF_HEADS = 8
DIFF_QK_DIM = 64
DIFF_V_DIM = 2 * DIFF_QK_DIM
DIFF_WIDTH = N_DIFF_HEADS * DIFF_V_DIM
N_NA_HEADS = 8
NA_HEAD_DIM = 128
NA_WIDTH = N_NA_HEADS * NA_HEAD_DIM
NA_KR = 8
NA_KC = 16
IN_COLS = 3 * DIFF_WIDTH + 3 * NA_WIDTH + 2 * D_MODEL
D_FF = 5632
LN_EPS = 1e-5
RMS_EPS = 1e-5
DEEPNORM_ALPHA = 2.0 ** 0.25
LAMBDA_INIT = 0.8 - 0.6 * math.exp(-0.3 * 0)
LOG2E = 1.4426950408889634

HEAD_COLS = 128
QA_BLK, KA_BLK, VA_BLK = 0, 8, 16
QB_BLK, KB_BLK, VB_BLK = 24, 32, 40
GATE_COL0 = 3 * DIFF_WIDTH + 3 * NA_WIDTH

MIB = 1024 * 1024
INPROJ_TM, INPROJ_TN = 1024, 1024
INPROJ_SUB_M = 512
DIFF_TQ = 1024
DIFF_SUB = 512
DIFF_ROWS = 16
NA_QROWS = 4
NA_KROWS = 12
NA_TQ = NA_QROWS * GRID_W
NA_TK = NA_KROWS * GRID_W
NA_GROUPS = GRID_ROWS // NA_QROWS
MERGE_TM = 512
FFN_TN = 512
FFN_NJ = D_FF // FFN_TN
FFN_TM = 512
F32_SUBLANES = 8
DOWN_TM = 512
DOWN_SUB_N = 512


def _cparams(n_axes, vmem_mib):
    return pltpu.CompilerParams(
        dimension_semantics=("arbitrary",) * n_axes,
        vmem_limit_bytes=vmem_mib * MIB,
    )


def _inproj_kernel(x_ref, w_ref, b_ref, s_ref, z_ref, xb_ref):
    j = pl.program_id(1)

    @pl.when(j == 0)
    def _():
        xb_ref[...] = x_ref[...].astype(BF16)

    def project(finish):
        halves = range(0, INPROJ_TM, INPROJ_SUB_M)
        accs = [jnp.dot(xb_ref[pl.ds(r0, INPROJ_SUB_M), :], w_ref[...], preferred_element_type=F32)
                for r0 in halves]
        for r0, acc in zip(halves, accs):
            z_ref[pl.ds(r0, INPROJ_SUB_M), :] = finish((acc + b_ref[...]) * s_ref[...]).astype(BF16)

    gate0 = GATE_COL0 // INPROJ_TN

    @pl.when(j < gate0)
    def _():
        project(lambda acc: acc)

    @pl.when(j >= gate0)
    def _():
        project(lambda acc: 0.5 * jnp.tanh(0.5 * acc) + 0.5)


def _in_proj(x2d, w_bf, b_row, scale_row):
    m = x2d.shape[0]
    return pl.pallas_call(
        _inproj_kernel,
        out_shape=jax.ShapeDtypeStruct((m, IN_COLS), BF16),
        grid=(m // INPROJ_TM, IN_COLS // INPROJ_TN),
        in_specs=[
            pl.BlockSpec((INPROJ_TM, D_MODEL), lambda i, j: (i, 0)),
            pl.BlockSpec((D_MODEL, INPROJ_TN), lambda i, j: (0, j)),
            pl.BlockSpec((1, INPROJ_TN), lambda i, j: (0, j)),
            pl.BlockSpec((1, INPROJ_TN), lambda i, j: (0, j)),
        ],
        out_specs=pl.BlockSpec((INPROJ_TM, INPROJ_TN), lambda i, j: (i, j)),
        scratch_shapes=[pltpu.VMEM((INPROJ_TM, D_MODEL), BF16)],
        compiler_params=_cparams(2, 48),
        name="in_proj",
    )(x2d, w_bf, b_row, scale_row)


def _diff_attn_kernel(slopes_ref, lq1_ref, lk1_ref, lq2_ref, lk2_ref, g_ref,
                      q_ref, k_ref, v_ref, o_ref, bias_ref, *sub_refs):
    n_sub = DIFF_TQ // DIFF_SUB
    s_refs, a_refs = sub_refs[:2 * n_sub], sub_refs[2 * n_sub:]
    h = pl.program_id(0)
    qi = pl.program_id(1)
    b = pl.program_id(2)

    @pl.when(b == 0)
    def _():
        rows = lax.broadcasted_iota(jnp.int32, (DIFF_TQ, SEQ), 0) + qi * DIFF_TQ
        cols = lax.broadcasted_iota(jnp.int32, (DIFF_TQ, SEQ), 1)
        dist = jnp.abs(rows - cols).astype(F32)
        bias_ref[...] = dist * (-LOG2E * slopes_ref[h])

    lam = (jnp.exp(jnp.sum(lq1_ref[...] * lk1_ref[...], axis=-1, keepdims=True))
           - jnp.exp(jnp.sum(lq2_ref[...] * lk2_ref[...], axis=-1, keepdims=True))
           + LAMBDA_INIT)

    k = k_ref[...]
    nt = (((1,), (1,)), ((), ()))
    subs = range(DIFF_TQ // DIFF_SUB)
    for t in subs:
        q = q_ref[pl.ds(t * DIFF_SUB, DIFF_SUB), :]
        lane = lax.broadcasted_iota(jnp.int32, q.shape, 1)
        zero = jnp.zeros_like(q)
        bias = bias_ref[pl.ds(t * DIFF_SUB, DIFF_SUB), :]
        s_refs[2 * t][...] = lax.dot_general(jnp.where(lane < DIFF_QK_DIM, q, zero), k, nt,
                                             preferred_element_type=F32) + bias
        s_refs[2 * t + 1][...] = lax.dot_general(jnp.where(lane >= DIFF_QK_DIM, q, zero), k, nt,
                                                 preferred_element_type=F32) + bias
    for t in subs:
        s1_ref, s2_ref, a_ref = s_refs[2 * t], s_refs[2 * t + 1], a_refs[t]
        inv_l1 = []
        for r0 in range(0, DIFF_SUB, DIFF_ROWS):
            halves = []
            for r in range(r0, r0 + DIFF_ROWS, F32_SUBLANES):
                rows = pl.ds(r, F32_SUBLANES)
                x1 = s1_ref[rows, :]
                p1 = jnp.exp2(x1 - jnp.max(x1, axis=-1, keepdims=True))
                l1 = jnp.sum(p1, axis=-1, keepdims=True)
                x2 = s2_ref[rows, :]
                p2 = jnp.exp2(x2 - jnp.max(x2, axis=-1, keepdims=True))
                l2 = jnp.sum(p2, axis=-1, keepdims=True)
                halves.append(p1 - p2 * (lam * l1 / l2))
                inv_l1.append(1.0 / l1)
            a_ref[pl.ds(r0, DIFF_ROWS), :] = jnp.concatenate(halves, axis=0).astype(BF16)
        o = jnp.dot(a_ref[...], v_ref[...], preferred_element_type=F32) * jnp.concatenate(inv_l1, axis=0)
        ms = jnp.mean(o * o, axis=-1, keepdims=True)
        o = o * lax.rsqrt(ms + RMS_EPS) * g_ref[...] * (1.0 - LAMBDA_INIT)
        o_ref[pl.ds(t * DIFF_SUB, DIFF_SUB), :] = o.astype(BF16)


def _diff_attn(z3, slopes, lq1, lk1, lq2, lk2, subln_g):
    bsz = z3.shape[0]
    vec = lambda n: pl.BlockSpec((1, n), lambda h, qi, b: (0, 0))
    return pl.pallas_call(
        _diff_attn_kernel,
        out_shape=jax.ShapeDtypeStruct((bsz, SEQ, DIFF_WIDTH), BF16),
        grid=(N_DIFF_HEADS, SEQ // DIFF_TQ, bsz),
        in_specs=[
            pl.BlockSpec(memory_space=pltpu.SMEM),
            vec(DIFF_QK_DIM), vec(DIFF_QK_DIM), vec(DIFF_QK_DIM), vec(DIFF_QK_DIM),
            vec(DIFF_V_DIM),
            pl.BlockSpec((None, DIFF_TQ, HEAD_COLS), lambda h, qi, b: (b, qi, QA_BLK + h)),
            pl.BlockSpec((None, SEQ, HEAD_COLS), lambda h, qi, b: (b, 0, KA_BLK + h)),
            pl.BlockSpec((None, SEQ, HEAD_COLS), lambda h, qi, b: (b, 0, VA_BLK + h)),
        ],
        out_specs=pl.BlockSpec((None, DIFF_TQ, HEAD_COLS), lambda h, qi, b: (b, qi, h)),
        scratch_shapes=([pltpu.VMEM((DIFF_TQ, SEQ), F32)]
                        + [pltpu.VMEM((DIFF_SUB, SEQ), F32)] * (2 * (DIFF_TQ // DIFF_SUB))
                        + [pltpu.VMEM((DIFF_SUB, SEQ), BF16)] * (DIFF_TQ // DIFF_SUB)),
        compiler_params=_cparams(3, 48),
        name="diff_attn",
    )(slopes, lq1, lk1, lq2, lk2, subln_g, z3, z3, z3)


def _na_key_row0(g):
    return min(max(g * NA_QROWS - NA_KR // 2, 0), GRID_ROWS - NA_KROWS)


def _na_table_id(g):
    return 0 if g == 0 else (2 if g == NA_GROUPS - 1 else 1)


def _na_bias_tables(rpb):
    n_dr, n_dc = 2 * NA_KR - 1, 2 * NA_KC - 1
    period = 2 * GRID_W
    w = jnp.pad(rpb.astype(F32), ((0, 0), (0, 0), (0, period - n_dc)))
    f = jnp.tile(w, (1, 1, GRID_W))[..., :GRID_W * (period - 1)]
    toe = f.reshape(N_NA_HEADS, n_dr, GRID_W, period - 1)[..., NA_KC - 1:NA_KC - 1 + GRID_W]
    toe = jnp.pad(toe.transpose(0, 2, 1, 3), ((0, 0), (0, 0), (NA_KROWS, NA_KROWS), (0, 0)))
    tabs = []
    for g in (0, 1, NA_GROUPS - 1):
        strips = []
        for rq in range(NA_QROWS):
            off = _na_key_row0(g) - (g * NA_QROWS + rq) + (NA_KR - 1) + NA_KROWS
            strips.append(toe[:, :, off:off + NA_KROWS, :].reshape(N_NA_HEADS, GRID_W, NA_TK))
        vals = jnp.stack(strips, axis=1).reshape(N_NA_HEADS, NA_TQ, NA_TK) * LOG2E
        qi, kj = np.arange(NA_TQ), np.arange(NA_TK)
        r, c = g * NA_QROWS + qi // GRID_W, qi % GRID_W
        kr, kc = _na_key_row0(g) + kj // GRID_W, kj % GRID_W
        row_start = np.clip(r - NA_KR // 2, 0, GRID_ROWS - NA_KR)
        col_start = np.clip(c - NA_KC // 2, 0, GRID_W - NA_KC)
        valid = ((kr[None, :] >= row_start[:, None]) & (kr[None, :] < row_start[:, None] + NA_KR)
                 & (kc[None, :] >= col_start[:, None]) & (kc[None, :] < col_start[:, None] + NA_KC))
        tabs.append(jnp.where(jnp.asarray(valid)[None], vals, -jnp.inf))
    return jnp.stack(tabs, axis=1)


def _na_kernel(bias_ref, q_ref, k_ref, v_ref, o_ref, *s_refs):
    nt = (((1,), (1,)), ((), ()))
    for g in range(NA_GROUPS):
        keys = pl.ds(_na_key_row0(g) * GRID_W, NA_TK)
        s = lax.dot_general(q_ref[pl.ds(g * NA_TQ, NA_TQ), :], k_ref[keys, :], nt, preferred_element_type=F32)
        s_refs[g][...] = s + bias_ref[_na_table_id(g)]
    for g in range(NA_GROUPS):
        rows = pl.ds(g * NA_TQ, NA_TQ)
        keys = pl.ds(_na_key_row0(g) * GRID_W, NA_TK)
        s = s_refs[g][...]
        p = jnp.exp2(s - jnp.max(s, axis=-1, keepdims=True))
        r = 1.0 / jnp.sum(p, axis=-1, keepdims=True)
        o = jnp.dot(p.astype(BF16), v_ref[keys, :], preferred_element_type=F32) * r
        o_ref[rows, :] = o.astype(BF16)


def _na_attn(z3, bias_tabs):
    bsz = z3.shape[0]
    head_block = lambda blk: pl.BlockSpec((None, SEQ, HEAD_COLS), lambda h, b: (b, 0, blk + h))
    return pl.pallas_call(
        _na_kernel,
        out_shape=jax.ShapeDtypeStruct((bsz, SEQ, NA_WIDTH), BF16),
        grid=(N_NA_HEADS, bsz),
        in_specs=[
            pl.BlockSpec((None, 3, NA_TQ, NA_TK), lambda h, b: (h, 0, 0, 0)),
            head_block(QB_BLK), head_block(KB_BLK), head_block(VB_BLK),
        ],
        out_specs=head_block(0),
        scratch_shapes=[pltpu.VMEM((NA_TQ, NA_TK), F32)] * NA_GROUPS,
        compiler_params=_cparams(2, 32),
        name="na_attn",
    )(bias_tabs, z3, z3, z3)


def _layer_norm(r, g, b):
    mu = jnp.mean(r, axis=-1, keepdims=True)
    d = r - mu
    var = jnp.mean(d * d, axis=-1, keepdims=True)
    return d * lax.rsqrt(var + LN_EPS) * g + b


def _merge_kernel(oa_ref, ob_ref, ga_ref, gb_ref, x_ref, wa_ref, wb_ref, wo_ref, g_ref, b_ref,
                  x1_ref, x1b_ref):
    ya = jnp.dot(oa_ref[...], wa_ref[...], preferred_element_type=F32)
    yb = jnp.dot(ob_ref[...], wb_ref[...], preferred_element_type=F32)
    y = ga_ref[...].astype(F32) * ya + gb_ref[...].astype(F32) * yb
    hproj = jnp.dot(y.astype(BF16), wo_ref[...], preferred_element_type=F32)
    x1 = _layer_norm(DEEPNORM_ALPHA * x_ref[...] + hproj, g_ref[...], b_ref[...])
    x1_ref[...] = x1
    x1b_ref[...] = x1.astype(BF16)


def _merge(oa2d, ob2d, z2d, x2d, wa, wb, wo, ln_g, ln_b):
    m = x2d.shape[0]
    gate_blk = GATE_COL0 // D_MODEL
    const = lambda shape: pl.BlockSpec(shape, lambda i: (0, 0), pipeline_mode=pl.Buffered(1))
    return pl.pallas_call(
        _merge_kernel,
        out_shape=(jax.ShapeDtypeStruct((m, D_MODEL), F32), jax.ShapeDtypeStruct((m, D_MODEL), BF16)),
        grid=(m // MERGE_TM,),
        in_specs=[
            pl.BlockSpec((MERGE_TM, DIFF_WIDTH), lambda i: (i, 0)),
            pl.BlockSpec((MERGE_TM, NA_WIDTH), lambda i: (i, 0)),
            pl.BlockSpec((MERGE_TM, D_MODEL), lambda i: (i, gate_blk)),
            pl.BlockSpec((MERGE_TM, D_MODEL), lambda i: (i, gate_blk + 1)),
            pl.BlockSpec((MERGE_TM, D_MODEL), lambda i: (i, 0)),
            const((DIFF_WIDTH, D_MODEL)),
            const((NA_WIDTH, D_MODEL)),
            const((D_MODEL, D_MODEL)),
            const((1, D_MODEL)),
            const((1, D_MODEL)),
        ],
        out_specs=(pl.BlockSpec((MERGE_TM, D_MODEL), lambda i: (i, 0)),
                   pl.BlockSpec((MERGE_TM, D_MODEL), lambda i: (i, 0))),
        compiler_params=_cparams(1, 56),
        name="merge",
    )(oa2d, ob2d, z2d, z2d, x2d, wa, wb, wo, ln_g, ln_b)


def _seq_conv(hmat, cw_ref, cb_ref):
    n, t = hmat.shape[0], F32_SUBLANES
    edge_row = lax.broadcasted_iota(jnp.int32, (t, hmat.shape[1]), 0)
    prev = pltpu.roll(hmat, 1, 0)
    nxt = pltpu.roll(hmat, n - 1, 0)
    prev = jnp.concatenate([jnp.where(edge_row == 0, 0.0, prev[:t]), prev[t:]], axis=0)
    nxt = jnp.concatenate([nxt[:n - t], jnp.where(edge_row == t - 1, 0.0, nxt[n - t:])], axis=0)
    return cb_ref[...] + prev * cw_ref[0:1, :] + hmat * cw_ref[1:2, :] + nxt * cw_ref[2:3, :]


def _row_chunked_dot(x_ref, w_ref):
    return jnp.concatenate(
        [jnp.dot(x_ref[pl.ds(r0, FFN_TM), :], w_ref[...], preferred_element_type=F32)
         for r0 in range(0, x_ref.shape[0], FFN_TM)], axis=0)


def _ffn_up_kernel(x_ref, wg_ref, wv_ref, cwg_ref, cwv_ref, cbg_ref, cbv_ref, o_ref):
    gate = _seq_conv(_row_chunked_dot(x_ref, wg_ref), cwg_ref, cbg_ref)
    val = _seq_conv(_row_chunked_dot(x_ref, wv_ref), cwv_ref, cbv_ref)
    gelu = 0.5 * gate * (1.0 + lax.erf(gate * (2.0 ** -0.5)))
    o_ref[...] = (gelu * val).astype(BF16)


def _ffn_up(x1b, w_up, conv_w, conv_b):
    m = x1b.shape[0]
    return pl.pallas_call(
        _ffn_up_kernel,
        out_shape=jax.ShapeDtypeStruct((m, D_FF), BF16),
        grid=(m // SEQ, FFN_NJ),
        in_specs=[
            pl.BlockSpec((SEQ, D_MODEL), lambda b, j: (b, 0)),
            pl.BlockSpec((D_MODEL, FFN_TN), lambda b, j: (0, j)),
            pl.BlockSpec((D_MODEL, FFN_TN), lambda b, j: (0, FFN_NJ + j)),
            pl.BlockSpec((3, FFN_TN), lambda b, j: (0, j)),
            pl.BlockSpec((3, FFN_TN), lambda b, j: (0, FFN_NJ + j)),
            pl.BlockSpec((1, FFN_TN), lambda b, j: (0, j)),
            pl.BlockSpec((1, FFN_TN), lambda b, j: (0, FFN_NJ + j)),
        ],
        out_specs=pl.BlockSpec((SEQ, FFN_TN), lambda b, j: (b, j)),
        compiler_params=_cparams(2, 56),
        name="ffn_up",
    )(x1b, w_up, w_up, conv_w, conv_w, conv_b, conv_b)


def _ffn_down_kernel(g_ref, w_ref, x1_ref, lg_ref, lb_ref, o_ref):
    g = g_ref[...]
    f = jnp.concatenate(
        [jnp.dot(g, w_ref[:, pl.ds(c0, DOWN_SUB_N)], preferred_element_type=F32)
         for c0 in range(0, D_MODEL, DOWN_SUB_N)], axis=1)
    o_ref[...] = _layer_norm(DEEPNORM_ALPHA * x1_ref[...] + f, lg_ref[...], lb_ref[...])


def _ffn_down(g2d, w_down, x1, ln_g, ln_b):
    m = g2d.shape[0]
    const = lambda shape: pl.BlockSpec(shape, lambda i: (0, 0), pipeline_mode=pl.Buffered(1))
    return pl.pallas_call(
        _ffn_down_kernel,
        out_shape=jax.ShapeDtypeStruct((m, D_MODEL), F32),
        grid=(m // DOWN_TM,),
        in_specs=[
            pl.BlockSpec((DOWN_TM, D_FF), lambda i: (i, 0)),
            const((D_FF, D_MODEL)),
            pl.BlockSpec((DOWN_TM, D_MODEL), lambda i: (i, 0)),
            const((1, D_MODEL)),
            const((1, D_MODEL)),
        ],
        out_specs=pl.BlockSpec((DOWN_TM, D_MODEL), lambda i: (i, 0)),
        compiler_params=_cparams(1, 60),
        name="ffn_down",
    )(g2d, w_down, x1, ln_g, ln_b)


def _z_col_scale():
    s = np.ones((1, IN_COLS), np.float32)
    s[0, :DIFF_WIDTH] = DIFF_QK_DIM ** -0.5 * LOG2E
    s[0, 3 * DIFF_WIDTH:3 * DIFF_WIDTH + NA_WIDTH] = NA_HEAD_DIM ** -0.5 * LOG2E
    return s


def _alibi_slopes():
    n = N_DIFF_HEADS
    return np.array([2.0 ** (-8.0 * (i + 1) / n) for i in range(n)], dtype=np.float32)


def kernel(x, w_in, b_in, lam_q1, lam_k1, lam_q2, lam_k2, subln_g, rpb, w_branch_a, w_branch_b, w_out,
           ln1_g, ln1_b, w_up, conv_w, conv_b, w_down, ln2_g, ln2_b):
    bsz, seq, d = x.shape
    assert (seq, d) == (SEQ, D_MODEL) and w_in.shape[0] == 1
    row = lambda a: a.reshape(1, -1).astype(F32)
    x2d = x.reshape(bsz * seq, d)

    z = _in_proj(x2d, w_in[0].astype(BF16), row(b_in[0]), jnp.asarray(_z_col_scale()))
    z3 = z.reshape(bsz, seq, IN_COLS)
    oa = _diff_attn(z3, jnp.asarray(_alibi_slopes()), row(lam_q1[0]), row(lam_k1[0]), row(lam_q2[0]),
                    row(lam_k2[0]), row(subln_g[0]))
    ob = _na_attn(z3, _na_bias_tables(rpb[0]))
    x1, x1b = _merge(oa.reshape(bsz * seq, DIFF_WIDTH), ob.reshape(bsz * seq, NA_WIDTH), z, x2d,
                     w_branch_a[0].astype(BF16), w_branch_b[0].astype(BF16), w_out[0].astype(BF16),
                     row(ln1_g[0]), row(ln1_b[0]))
    g = _ffn_up(x1b, w_up[0].astype(BF16), conv_w[0].astype(F32), row(conv_b[0]))
    out = _ffn_down(g, w_down[0].astype(BF16), x1, row(ln2_g[0]), row(ln2_b[0]))
    return out.reshape(bsz, seq, d)
```

```python
import math
from functools import partial

import jax
import jax.numpy as jnp
import numpy as np
from jax import lax
from jax.experimental import pallas as pl
from jax.experimental.pallas import tpu as pltpu

F32 = jnp.float32
BF16 = jnp.bfloat16

D_MODEL = 2048
SEQ = 2048
GRID_W = 64
GRID_ROWS = SEQ // GRID_W
N_DIFF_HEADS = 8
DIFF_QK_DIM = 64
DIFF_V_DIM = 2 * DIFF_QK_DIM
DIFF_WIDTH = N_DIFF_HEADS * DIFF_V_DIM
N_NA_HEADS = 8
NA_HEAD_DIM = 128
NA_WIDTH = N_NA_HEADS * NA_HEAD_DIM
NA_KR = 8
NA_KC = 16
IN_COLS = 3 * DIFF_WIDTH + 3 * NA_WIDTH + 2 * D_MODEL
D_FF = 5632
LN_EPS = 1e-5
RMS_EPS = 1e-5
DEEPNORM_ALPHA = 2.0 ** 0.25
LAMBDA_INIT = 0.8 - 0.6 * math.exp(-0.3 * 0)
LOG2E = 1.4426950408889634

HEAD_COLS = 128
QA_BLK, KA_BLK, VA_BLK = 0, 8, 16
QB_BLK, KB_BLK, VB_BLK = 24, 32, 40
GATE_COL0 = 3 * DIFF_WIDTH + 3 * NA_WIDTH

MIB = 1024 * 1024
INPROJ_TM, INPROJ_TN = 1024, 1024
INPROJ_SUB_N = 256
DIFF_TQ = 1024
DIFF_SUB = 512
DIFF_ROWS = 16
NA_QROWS = 4
NA_KROWS = 12
NA_TQ = NA_QROWS * GRID_W
NA_TK = NA_KROWS * GRID_W
NA_GROUPS = GRID_ROWS // NA_QROWS
MERGE_TM = 512
FFN_TN = 512
FFN_NJ = D_FF // FFN_TN
FFN_TM = 1024
FFN_SUB_N = 256
F32_SUBLANES = 8
DOWN_TM = 512
DOWN_SUB_N = 512


def _cparams(n_axes, vmem_mib):
    return pltpu.CompilerParams(
        dimension_semantics=("arbitrary",) * n_axes,
        vmem_limit_bytes=vmem_mib * MIB,
    )


def _inproj_kernel(x_ref, w_ref, b_ref, s_ref, z_ref, xb_ref):
    j = pl.program_id(1)

    @pl.when(j == 0)
    def _():
        xb_ref[...] = x_ref[...].astype(BF16)

    def project(finish):
        pieces = [pl.ds(c0, INPROJ_SUB_N) for c0 in range(0, INPROJ_TN, INPROJ_SUB_N)]
        x = xb_ref[...]
        accs = [jnp.dot(x, w_ref[:, cols], preferred_element_type=F32) for cols in pieces]
        for cols, acc in zip(pieces, accs):
            z_ref[:, cols] = finish((acc + b_ref[:, cols]) * s_ref[:, cols]).astype(BF16)

    gate0 = GATE_COL0 // INPROJ_TN

    @pl.when(j < gate0)
    def _():
        project(lambda acc: acc)

    @pl.when(j >= gate0)
    def _():
        project(lambda acc: 0.5 * jnp.tanh(0.5 * acc) + 0.5)


def _in_proj(x2d, w_bf, b_row, scale_row):
    m = x2d.shape[0]
    return pl.pallas_call(
        _inproj_kernel,
        out_shape=jax.ShapeDtypeStruct((m, IN_COLS), BF16),
        grid=(m // INPROJ_TM, IN_COLS // INPROJ_TN),
        in_specs=[
            pl.BlockSpec((INPROJ_TM, D_MODEL), lambda i, j: (i, 0)),
            pl.BlockSpec((D_MODEL, INPROJ_TN), lambda i, j: (0, j)),
            pl.BlockSpec((1, INPROJ_TN), lambda i, j: (0, j)),
            pl.BlockSpec((1, INPROJ_TN), lambda i, j: (0, j)),
        ],
        out_specs=pl.BlockSpec((INPROJ_TM, INPROJ_TN), lambda i, j: (i, j)),
        scratch_shapes=[pltpu.VMEM((INPROJ_TM, D_MODEL), BF16)],
        compiler_params=_cparams(2, 48),
        name="in_proj",
    )(x2d, w_bf, b_row, scale_row)


def _diff_attn_kernel(slopes_ref, lq1_ref, lk1_ref, lq2_ref, lk2_ref, g_ref,
                      q_ref, k_ref, v_ref, o_ref, bias_ref, *sub_refs):
    n_sub = DIFF_TQ // DIFF_SUB
    s_refs, a_refs = sub_refs[:2 * n_sub], sub_refs[2 * n_sub:]
    h = pl.program_id(0)
    qi = pl.program_id(1)
    b = pl.program_id(2)

    @pl.when(b == 0)
    def _():
        rows = lax.broadcasted_iota(jnp.int32, (DIFF_TQ, SEQ), 0) + qi * DIFF_TQ
        cols = lax.broadcasted_iota(jnp.int32, (DIFF_TQ, SEQ), 1)
        dist = jnp.abs(rows - cols).astype(F32)
        bias_ref[...] = dist * (-LOG2E * slopes_ref[h])

    lam = (jnp.exp(jnp.sum(lq1_ref[...] * lk1_ref[...], axis=-1, keepdims=True))
           - jnp.exp(jnp.sum(lq2_ref[...] * lk2_ref[...], axis=-1, keepdims=True))
           + LAMBDA_INIT)

    k = k_ref[...]
    nt = (((1,), (1,)), ((), ()))
    subs = range(DIFF_TQ // DIFF_SUB)
    for t in subs:
        q = q_ref[pl.ds(t * DIFF_SUB, DIFF_SUB), :]
        lane = lax.broadcasted_iota(jnp.int32, q.shape, 1)
        zero = jnp.zeros_like(q)
        bias = bias_ref[pl.ds(t * DIFF_SUB, DIFF_SUB), :]
        s_refs[2 * t][...] = lax.dot_general(jnp.where(lane < DIFF_QK_DIM, q, zero), k, nt,
                                             preferred_element_type=F32) + bias
        s_refs[2 * t + 1][...] = lax.dot_general(jnp.where(lane >= DIFF_QK_DIM, q, zero), k, nt,
                                                 preferred_element_type=F32) + bias
    for t in subs:
        s1_ref, s2_ref, a_ref = s_refs[2 * t], s_refs[2 * t + 1], a_refs[t]
        inv_l1 = []
        for r0 in range(0, DIFF_SUB, DIFF_ROWS):
            halves = []
            for r in range(r0, r0 + DIFF_ROWS, F32_SUBLANES):
                rows = pl.ds(r, F32_SUBLANES)
                x1 = s1_ref[rows, :]
                p1 = jnp.exp2(x1 - jnp.max(x1, axis=-1, keepdims=True))
                l1 = jnp.sum(p1, axis=-1, keepdims=True)
                x2 = s2_ref[rows, :]
                p2 = jnp.exp2(x2 - jnp.max(x2, axis=-1, keepdims=True))
                l2 = jnp.sum(p2, axis=-1, keepdims=True)
                halves.append(p1 - p2 * (lam * l1 / l2))
                inv_l1.append(1.0 / l1)
            a_ref[pl.ds(r0, DIFF_ROWS), :] = jnp.concatenate(halves, axis=0).astype(BF16)
        o = jnp.dot(a_ref[...], v_ref[...], preferred_element_type=F32) * jnp.concatenate(inv_l1, axis=0)
        ms = jnp.mean(o * o, axis=-1, keepdims=True)
        o = o * lax.rsqrt(ms + RMS_EPS) * g_ref[...] * (1.0 - LAMBDA_INIT)
        o_ref[pl.ds(t * DIFF_SUB, DIFF_SUB), :] = o.astype(BF16)


def _diff_attn(z3, slopes, lq1, lk1, lq2, lk2, subln_g):
    bsz = z3.shape[0]
    vec = lambda n: pl.BlockSpec((1, n), lambda h, qi, b: (0, 0))
    return pl.pallas_call(
        _diff_attn_kernel,
        out_shape=jax.ShapeDtypeStruct((bsz, SEQ, DIFF_WIDTH), BF16),
        grid=(N_DIFF_HEADS, SEQ // DIFF_TQ, bsz),
        in_specs=[
            pl.BlockSpec(memory_space=pltpu.SMEM),
            vec(DIFF_QK_DIM), vec(DIFF_QK_DIM), vec(DIFF_QK_DIM), vec(DIFF_QK_DIM),
            vec(DIFF_V_DIM),
            pl.BlockSpec((None, DIFF_TQ, HEAD_COLS), lambda h, qi, b: (b, qi, QA_BLK + h)),
            pl.BlockSpec((None, SEQ, HEAD_COLS), lambda h, qi, b: (b, 0, KA_BLK + h)),
            pl.BlockSpec((None, SEQ, HEAD_COLS), lambda h, qi, b: (b, 0, VA_BLK + h)),
        ],
        out_specs=pl.BlockSpec((None, DIFF_TQ, HEAD_COLS), lambda h, qi, b: (b, qi, h)),
        scratch_shapes=([pltpu.VMEM((DIFF_TQ, SEQ), F32)]
                        + [pltpu.VMEM((DIFF_SUB, SEQ), F32)] * (2 * (DIFF_TQ // DIFF_SUB))
                        + [pltpu.VMEM((DIFF_SUB, SEQ), BF16)] * (DIFF_TQ // DIFF_SUB)),
        compiler_params=_cparams(3, 48),
        name="diff_attn",
    )(slopes, lq1, lk1, lq2, lk2, subln_g, z3, z3, z3)


def _na_key_row0(g):
    return min(max(g * NA_QROWS - NA_KR // 2, 0), GRID_ROWS - NA_KROWS)


def _na_table_id(g):
    return 0 if g == 0 else (2 if g == NA_GROUPS - 1 else 1)


def _na_bias_tables(rpb):
    n_dr, n_dc = 2 * NA_KR - 1, 2 * NA_KC - 1
    period = 2 * GRID_W
    w = jnp.pad(rpb.astype(F32), ((0, 0), (0, 0), (0, period - n_dc)))
    f = jnp.tile(w, (1, 1, GRID_W))[..., :GRID_W * (period - 1)]
    toe = f.reshape(N_NA_HEADS, n_dr, GRID_W, period - 1)[..., NA_KC - 1:NA_KC - 1 + GRID_W]
    toe = jnp.pad(toe.transpose(0, 2, 1, 3), ((0, 0), (0, 0), (NA_KROWS, NA_KROWS), (0, 0)))
    tabs = []
    for g in (0, 1, NA_GROUPS - 1):
        strips = []
        for rq in range(NA_QROWS):
            off = _na_key_row0(g) - (g * NA_QROWS + rq) + (NA_KR - 1) + NA_KROWS
            strips.append(toe[:, :, off:off + NA_KROWS, :].reshape(N_NA_HEADS, GRID_W, NA_TK))
        vals = jnp.stack(strips, axis=1).reshape(N_NA_HEADS, NA_TQ, NA_TK) * LOG2E
        qi, kj = np.arange(NA_TQ), np.arange(NA_TK)
        r, c = g * NA_QROWS + qi // GRID_W, qi % GRID_W
        kr, kc = _na_key_row0(g) + kj // GRID_W, kj % GRID_W
        row_start = np.clip(r - NA_KR // 2, 0, GRID_ROWS - NA_KR)
        col_start = np.clip(c - NA_KC // 2, 0, GRID_W - NA_KC)
        valid = ((kr[None, :] >= row_start[:, None]) & (kr[None, :] < row_start[:, None] + NA_KR)
                 & (kc[None, :] >= col_start[:, None]) & (kc[None, :] < col_start[:, None] + NA_KC))
        tabs.append(jnp.where(jnp.asarray(valid)[None], vals, -jnp.inf))
    return jnp.stack(tabs, axis=1)


def _na_kernel(bias_ref, q_ref, k_ref, v_ref, o_ref, *s_refs):
    nt = (((1,), (1,)), ((), ()))
    for g in range(NA_GROUPS):
        keys = pl.ds(_na_key_row0(g) * GRID_W, NA_TK)
        s = lax.dot_general(q_ref[pl.ds(g * NA_TQ, NA_TQ), :], k_ref[keys, :], nt, preferred_element_type=F32)
        s_refs[g][...] = s + bias_ref[_na_table_id(g)]
    for g in range(NA_GROUPS):
        rows = pl.ds(g * NA_TQ, NA_TQ)
        keys = pl.ds(_na_key_row0(g) * GRID_W, NA_TK)
        s = s_refs[g][...]
        p = jnp.exp2(s - jnp.max(s, axis=-1, keepdims=True))
        r = 1.0 / jnp.sum(p, axis=-1, keepdims=True)
        o = jnp.dot(p.astype(BF16), v_ref[keys, :], preferred_element_type=F32) * r
        o_ref[rows, :] = o.astype(BF16)


def _na_attn(z3, bias_tabs):
    bsz = z3.shape[0]
    head_block = lambda blk: pl.BlockSpec((None, SEQ, HEAD_COLS), lambda h, b: (b, 0, blk + h))
    return pl.pallas_call(
        _na_kernel,
        out_shape=jax.ShapeDtypeStruct((bsz, SEQ, NA_WIDTH), BF16),
        grid=(N_NA_HEADS, bsz),
        in_specs=[
            pl.BlockSpec((None, 3, NA_TQ, NA_TK), lambda h, b: (h, 0, 0, 0)),
            head_block(QB_BLK), head_block(KB_BLK), head_block(VB_BLK),
        ],
        out_specs=head_block(0),
        scratch_shapes=[pltpu.VMEM((NA_TQ, NA_TK), F32)] * NA_GROUPS,
        compiler_params=_cparams(2, 32),
        name="na_attn",
    )(bias_tabs, z3, z3, z3)


def _layer_norm(r, g, b):
    mu = jnp.mean(r, axis=-1, keepdims=True)
    d = r - mu
    var = jnp.mean(d * d, axis=-1, keepdims=True)
    return d * lax.rsqrt(var + LN_EPS) * g + b


def _merge_kernel(oa_ref, ob_ref, ga_ref, gb_ref, x_ref, wa_ref, wb_ref, wo_ref, g_ref, b_ref,
                  x1_ref, x1b_ref):
    ya = jnp.dot(oa_ref[...], wa_ref[...], preferred_element_type=F32)
    yb = jnp.dot(ob_ref[...], wb_ref[...], preferred_element_type=F32)
    y = ga_ref[...].astype(F32) * ya + gb_ref[...].astype(F32) * yb
    hproj = jnp.dot(y.astype(BF16), wo_ref[...], preferred_element_type=F32)
    x1 = _layer_norm(DEEPNORM_ALPHA * x_ref[...] + hproj, g_ref[...], b_ref[...])
    x1_ref[...] = x1
    x1b_ref[...] = x1.astype(BF16)


def _merge(oa2d, ob2d, z2d, x2d, wa, wb, wo, ln_g, ln_b):
    m = x2d.shape[0]
    gate_blk = GATE_COL0 // D_MODEL
    const = lambda shape: pl.BlockSpec(shape, lambda i: (0, 0), pipeline_mode=pl.Buffered(1))
    return pl.pallas_call(
        _merge_kernel,
        out_shape=(jax.ShapeDtypeStruct((m, D_MODEL), F32), jax.ShapeDtypeStruct((m, D_MODEL), BF16)),
        grid=(m // MERGE_TM,),
        in_specs=[
            pl.BlockSpec((MERGE_TM, DIFF_WIDTH), lambda i: (i, 0)),
            pl.BlockSpec((MERGE_TM, NA_WIDTH), lambda i: (i, 0)),
            pl.BlockSpec((MERGE_TM, D_MODEL), lambda i: (i, gate_blk)),
            pl.BlockSpec((MERGE_TM, D_MODEL), lambda i: (i, gate_blk + 1)),
            pl.BlockSpec((MERGE_TM, D_MODEL), lambda i: (i, 0)),
            const((DIFF_WIDTH, D_MODEL)),
            const((NA_WIDTH, D_MODEL)),
            const((D_MODEL, D_MODEL)),
            const((1, D_MODEL)),
            const((1, D_MODEL)),
        ],
        out_specs=(pl.BlockSpec((MERGE_TM, D_MODEL), lambda i: (i, 0)),
                   pl.BlockSpec((MERGE_TM, D_MODEL), lambda i: (i, 0))),
        compiler_params=_cparams(1, 56),
        name="merge",
    )(oa2d, ob2d, z2d, z2d, x2d, wa, wb, wo, ln_g, ln_b)


def _seq_conv(hmat, cw_ref, cb_ref):
    n, t = hmat.shape[0], F32_SUBLANES
    edge_row = lax.broadcasted_iota(jnp.int32, (t, hmat.shape[1]), 0)
    prev = pltpu.roll(hmat, 1, 0)
    nxt = pltpu.roll(hmat, n - 1, 0)
    prev = jnp.concatenate([jnp.where(edge_row == 0, 0.0, prev[:t]), prev[t:]], axis=0)
    nxt = jnp.concatenate([nxt[:n - t], jnp.where(edge_row == t - 1, 0.0, nxt[n - t:])], axis=0)
    return cb_ref[...] + prev * cw_ref[0:1, :] + hmat * cw_ref[1:2, :] + nxt * cw_ref[2:3, :]


def _row_chunked_dot(x_ref, w_ref):
    return jnp.concatenate(
        [jnp.concatenate(
            [jnp.dot(x_ref[pl.ds(r0, FFN_TM), :], w_ref[:, pl.ds(c0, FFN_SUB_N)], preferred_element_type=F32)
             for c0 in range(0, w_ref.shape[1], FFN_SUB_N)], axis=1)
         for r0 in range(0, x_ref.shape[0], FFN_TM)], axis=0)


def _ffn_up_kernel(x_ref, wg_ref, wv_ref, cwg_ref, cwv_ref, cbg_ref, cbv_ref, o_ref):
    gate = _seq_conv(_row_chunked_dot(x_ref, wg_ref), cwg_ref, cbg_ref)
    val = _seq_conv(_row_chunked_dot(x_ref, wv_ref), cwv_ref, cbv_ref)
    gelu = 0.5 * gate * (1.0 + lax.erf(gate * (2.0 ** -0.5)))
    o_ref[...] = (gelu * val).astype(BF16)


def _ffn_up(x1b, w_up, conv_w, conv_b):
    m = x1b.shape[0]
    return pl.pallas_call(
        _ffn_up_kernel,
        out_shape=jax.ShapeDtypeStruct((m, D_FF), BF16),
        grid=(m // SEQ, FFN_NJ),
        in_specs=[
            pl.BlockSpec((SEQ, D_MODEL), lambda b, j: (b, 0)),
            pl.BlockSpec((D_MODEL, FFN_TN), lambda b, j: (0, j)),
            pl.BlockSpec((D_MODEL, FFN_TN), lambda b, j: (0, FFN_NJ + j)),
            pl.BlockSpec((3, FFN_TN), lambda b, j: (0, j)),
            pl.BlockSpec((3, FFN_TN), lambda b, j: (0, FFN_NJ + j)),
            pl.BlockSpec((1, FFN_TN), lambda b, j: (0, j)),
            pl.BlockSpec((1, FFN_TN), lambda b, j: (0, FFN_NJ + j)),
        ],
        out_specs=pl.BlockSpec((SEQ, FFN_TN), lambda b, j: (b, j)),
        compiler_params=_cparams(2, 56),
        name="ffn_up",
    )(x1b, w_up, w_up, conv_w, conv_w, conv_b, conv_b)


def _ffn_down_kernel(g_ref, w_ref, x1_ref, lg_ref, lb_ref, o_ref):
    g = g_ref[...]
    f = jnp.concatenate(
        [jnp.dot(g, w_ref[:, pl.ds(c0, DOWN_SUB_N)], preferred_element_type=F32)
         for c0 in range(0, D_MODEL, DOWN_SUB_N)], axis=1)
    o_ref[...] = _layer_norm(DEEPNORM_ALPHA * x1_ref[...] + f, lg_ref[...], lb_ref[...])


def _ffn_down(g2d, w_down, x1, ln_g, ln_b):
    m = g2d.shape[0]
    const = lambda shape: pl.BlockSpec(shape, lambda i: (0, 0), pipeline_mode=pl.Buffered(1))
    return pl.pallas_call(
        _ffn_down_kernel,
        out_shape=jax.ShapeDtypeStruct((m, D_MODEL), F32),
        grid=(m // DOWN_TM,),
        in_specs=[
            pl.BlockSpec((DOWN_TM, D_FF), lambda i: (i, 0)),
            const((D_FF, D_MODEL)),
            pl.BlockSpec((DOWN_TM, D_MODEL), lambda i: (i, 0)),
            const((1, D_MODEL)),
            const((1, D_MODEL)),
        ],
        out_specs=pl.BlockSpec((DOWN_TM, D_MODEL), lambda i: (i, 0)),
        compiler_params=_cparams(1, 60),
        name="ffn_down",
    )(g2d, w_down, x1, ln_g, ln_b)


def _z_col_scale():
    s = np.ones((1, IN_COLS), np.float32)
    s[0, :DIFF_WIDTH] = DIFF_QK_DIM ** -0.5 * LOG2E
    s[0, 3 * DIFF_WIDTH:3 * DIFF_WIDTH + NA_WIDTH] = NA_HEAD_DIM ** -0.5 * LOG2E
    return s


def _alibi_slopes():
    n = N_DIFF_HEADS
    return np.array([2.0 ** (-8.0 * (i + 1) / n) for i in range(n)], dtype=np.float32)


def kernel(x, w_in, b_in, lam_q1, lam_k1, lam_q2, lam_k2, subln_g, rpb, w_branch_a, w_branch_b, w_out,
           ln1_g, ln1_b, w_up, conv_w, conv_b, w_down, ln2_g, ln2_b):
    bsz, seq, d = x.shape
    assert (seq, d) == (SEQ, D_MODEL) and w_in.shape[0] == 1
    row = lambda a: a.reshape(1, -1).astype(F32)
    x2d = x.reshape(bsz * seq, d)

    z = _in_proj(x2d, w_in[0].astype(BF16), row(b_in[0]), jnp.asarray(_z_col_scale()))
    z3 = z.reshape(bsz, seq, IN_COLS)
    oa = _diff_attn(z3, jnp.asarray(_alibi_slopes()), row(lam_q1[0]), row(lam_k1[0]), row(lam_q2[0]),
                    row(lam_k2[0]), row(subln_g[0]))
    ob = _na_attn(z3, _na_bias_tables(rpb[0]))
    x1, x1b = _merge(oa.reshape(bsz * seq, DIFF_WIDTH), ob.reshape(bsz * seq, NA_WIDTH), z, x2d,
                     w_branch_a[0].astype(BF16), w_branch_b[0].astype(BF16), w_out[0].astype(BF16),
                     row(ln1_g[0]), row(ln1_b[0]))
    g = _ffn_up(x1b, w_up[0].astype(BF16), conv_w[0].astype(F32), row(conv_b[0]))
    out = _ffn_down(g, w_down[0].astype(BF16), x1, row(ln2_g[0]), row(ln2_b[0]))
    return out.reshape(bsz, seq, d)
```

```python
import math
from functools import partial

import jax
import jax.numpy as jnp
import numpy as np
from jax import lax
from jax.experimental import pallas as pl
from jax.experimental.pallas import tpu as pltpu

F32 = jnp.float32
BF16 = jnp.bfloat16

D_MODEL = 2048
SEQ = 2048
GRID_W = 64
GRID_ROWS = SEQ // GRID_W
N_DIFF_HEADS = 8
DIFF_QK_DIM = 64
DIFF_V_DIM = 2 * DIFF_QK_DIM
DIFF_WIDTH = N_DIFF_HEADS * DIFF_V_DIM
N_NA_HEADS = 8
NA_HEAD_DIM = 128
NA_WIDTH = N_NA_HEADS * NA_HEAD_DIM
NA_KR = 8
NA_KC = 16
IN_COLS = 3 * DIFF_WIDTH + 3 * NA_WIDTH + 2 * D_MODEL
D_FF = 5632
LN_EPS = 1e-5
RMS_EPS = 1e-5
DEEPNORM_ALPHA = 2.0 ** 0.25
LAMBDA_INIT = 0.8 - 0.6 * math.exp(-0.3 * 0)
LOG2E = 1.4426950408889634

HEAD_COLS = 128
QA_BLK, KA_BLK, VA_BLK = 0, 8, 16
QB_BLK, KB_BLK, VB_BLK = 24, 32, 40
GATE_COL0 = 3 * DIFF_WIDTH + 3 * NA_WIDTH

MIB = 1024 * 1024
INPROJ_TM, INPROJ_TN = 1024, 2048
INPROJ_SUB_M = 512
DIFF_TQ = 1024
DIFF_SUB = 512
DIFF_ROWS = 16
NA_QROWS = 4
NA_KROWS = 12
NA_TQ = NA_QROWS * GRID_W
NA_TK = NA_KROWS * GRID_W
NA_GROUPS = GRID_ROWS // NA_QROWS
MERGE_TM = 512
FFN_TN = 512
FFN_NJ = D_FF // FFN_TN
FFN_TM = 512
F32_SUBLANES = 8
DOWN_TM = 512
DOWN_SUB_N = 512


def _cparams(n_axes, vmem_mib):
    return pltpu.CompilerParams(
        dimension_semantics=("arbitrary",) * n_axes,
        vmem_limit_bytes=vmem_mib * MIB,
    )


def _inproj_kernel(x_ref, w_ref, b_ref, s_ref, z_ref, xb_ref):
    j = pl.program_id(1)

    @pl.when(j == 0)
    def _():
        xb_ref[...] = x_ref[...].astype(BF16)

    def project(finish):
        halves = range(0, INPROJ_TM, INPROJ_SUB_M)
        accs = [jnp.dot(xb_ref[pl.ds(r0, INPROJ_SUB_M), :], w_ref[...], preferred_element_type=F32)
                for r0 in halves]
        for r0, acc in zip(halves, accs):
            z_ref[pl.ds(r0, INPROJ_SUB_M), :] = finish((acc + b_ref[...]) * s_ref[...]).astype(BF16)

    gate0 = GATE_COL0 // INPROJ_TN

    @pl.when(j < gate0)
    def _():
        project(lambda acc: acc)

    @pl.when(j >= gate0)
    def _():
        project(lambda acc: 0.5 * jnp.tanh(0.5 * acc) + 0.5)


def _in_proj(x2d, w_bf, b_row, scale_row):
    m = x2d.shape[0]
    return pl.pallas_call(
        _inproj_kernel,
        out_shape=jax.ShapeDtypeStruct((m, IN_COLS), BF16),
        grid=(m // INPROJ_TM, IN_COLS // INPROJ_TN),
        in_specs=[
            pl.BlockSpec((INPROJ_TM, D_MODEL), lambda i, j: (i, 0)),
            pl.BlockSpec((D_MODEL, INPROJ_TN), lambda i, j: (0, j)),
            pl.BlockSpec((1, INPROJ_TN), lambda i, j: (0, j)),
            pl.BlockSpec((1, INPROJ_TN), lambda i, j: (0, j)),
        ],
        out_specs=pl.BlockSpec((INPROJ_TM, INPROJ_TN), lambda i, j: (i, j)),
        scratch_shapes=[pltpu.VMEM((INPROJ_TM, D_MODEL), BF16)],
        compiler_params=_cparams(2, 56),
        name="in_proj",
    )(x2d, w_bf, b_row, scale_row)


def _diff_attn_kernel(slopes_ref, lq1_ref, lk1_ref, lq2_ref, lk2_ref, g_ref,
                      q_ref, k_ref, v_ref, o_ref, bias_ref, *sub_refs):
    n_sub = DIFF_TQ // DIFF_SUB
    s_refs, a_refs = sub_refs[:2 * n_sub], sub_refs[2 * n_sub:]
    h = pl.program_id(0)
    qi = pl.program_id(1)
    b = pl.program_id(2)

    @pl.when(b == 0)
    def _():
        rows = lax.broadcasted_iota(jnp.int32, (DIFF_TQ, SEQ), 0) + qi * DIFF_TQ
        cols = lax.broadcasted_iota(jnp.int32, (DIFF_TQ, SEQ), 1)
        dist = jnp.abs(rows - cols).astype(F32)
        bias_ref[...] = dist * (-LOG2E * slopes_ref[h])

    lam = (jnp.exp(jnp.sum(lq1_ref[...] * lk1_ref[...], axis=-1, keepdims=True))
           - jnp.exp(jnp.sum(lq2_ref[...] * lk2_ref[...], axis=-1, keepdims=True))
           + LAMBDA_INIT)

    k = k_ref[...]
    nt = (((1,), (1,)), ((), ()))
    subs = range(DIFF_TQ // DIFF_SUB)
    for t in subs:
        q = q_ref[pl.ds(t * DIFF_SUB, DIFF_SUB), :]
        lane = lax.broadcasted_iota(jnp.int32, q.shape, 1)
        zero = jnp.zeros_like(q)
        bias = bias_ref[pl.ds(t * DIFF_SUB, DIFF_SUB), :]
        s_refs[2 * t][...] = lax.dot_general(jnp.where(lane < DIFF_QK_DIM, q, zero), k, nt,
                                             preferred_element_type=F32) + bias
        s_refs[2 * t + 1][...] = lax.dot_general(jnp.where(lane >= DIFF_QK_DIM, q, zero), k, nt,
                                                 preferred_element_type=F32) + bias
    for t in subs:
        s1_ref, s2_ref, a_ref = s_refs[2 * t], s_refs[2 * t + 1], a_refs[t]
        inv_l1 = []
        for r0 in range(0, DIFF_SUB, DIFF_ROWS):
            halves = []
            for r in range(r0, r0 + DIFF_ROWS, F32_SUBLANES):
                rows = pl.ds(r, F32_SUBLANES)
                x1 = s1_ref[rows, :]
                p1 = jnp.exp2(x1 - jnp.max(x1, axis=-1, keepdims=True))
                l1 = jnp.sum(p1, axis=-1, keepdims=True)
                x2 = s2_ref[rows, :]
                p2 = jnp.exp2(x2 - jnp.max(x2, axis=-1, keepdims=True))
                l2 = jnp.sum(p2, axis=-1, keepdims=True)
                halves.append(p1 - p2 * (lam * l1 / l2))
                inv_l1.append(1.0 / l1)
            a_ref[pl.ds(r0, DIFF_ROWS), :] = jnp.concatenate(halves, axis=0).astype(BF16)
        o = jnp.dot(a_ref[...], v_ref[...], preferred_element_type=F32) * jnp.concatenate(inv_l1, axis=0)
        ms = jnp.mean(o * o, axis=-1, keepdims=True)
        o = o * lax.rsqrt(ms + RMS_EPS) * g_ref[...] * (1.0 - LAMBDA_INIT)
        o_ref[pl.ds(t * DIFF_SUB, DIFF_SUB), :] = o.astype(BF16)


def _diff_attn(z3, slopes, lq1, lk1, lq2, lk2, subln_g):
    bsz = z3.shape[0]
    vec = lambda n: pl.BlockSpec((1, n), lambda h, qi, b: (0, 0))
    return pl.pallas_call(
        _diff_attn_kernel,
        out_shape=jax.ShapeDtypeStruct((bsz, SEQ, DIFF_WIDTH), BF16),
        grid=(N_DIFF_HEADS, SEQ // DIFF_TQ, bsz),
        in_specs=[
            pl.BlockSpec(memory_space=pltpu.SMEM),
            vec(DIFF_QK_DIM), vec(DIFF_QK_DIM), vec(DIFF_QK_DIM), vec(DIFF_QK_DIM),
            vec(DIFF_V_DIM),
            pl.BlockSpec((None, DIFF_TQ, HEAD_COLS), lambda h, qi, b: (b, qi, QA_BLK + h)),
            pl.BlockSpec((None, SEQ, HEAD_COLS), lambda h, qi, b: (b, 0, KA_BLK + h)),
            pl.BlockSpec((None, SEQ, HEAD_COLS), lambda h, qi, b: (b, 0, VA_BLK + h)),
        ],
        out_specs=pl.BlockSpec((None, DIFF_TQ, HEAD_COLS), lambda h, qi, b: (b, qi, h)),
        scratch_shapes=([pltpu.VMEM((DIFF_TQ, SEQ), F32)]
                        + [pltpu.VMEM((DIFF_SUB, SEQ), F32)] * (2 * (DIFF_TQ // DIFF_SUB))
                        + [pltpu.VMEM((DIFF_SUB, SEQ), BF16)] * (DIFF_TQ // DIFF_SUB)),
        compiler_params=_cparams(3, 48),
        name="diff_attn",
    )(slopes, lq1, lk1, lq2, lk2, subln_g, z3, z3, z3)


def _na_key_row0(g):
    return min(max(g * NA_QROWS - NA_KR // 2, 0), GRID_ROWS - NA_KROWS)


def _na_table_id(g):
    return 0 if g == 0 else (2 if g == NA_GROUPS - 1 else 1)


def _na_bias_tables(rpb):
    n_dr, n_dc = 2 * NA_KR - 1, 2 * NA_KC - 1
    period = 2 * GRID_W
    w = jnp.pad(rpb.astype(F32), ((0, 0), (0, 0), (0, period - n_dc)))
    f = jnp.tile(w, (1, 1, GRID_W))[..., :GRID_W * (period - 1)]
    toe = f.reshape(N_NA_HEADS, n_dr, GRID_W, period - 1)[..., NA_KC - 1:NA_KC - 1 + GRID_W]
    toe = jnp.pad(toe.transpose(0, 2, 1, 3), ((0, 0), (0, 0), (NA_KROWS, NA_KROWS), (0, 0)))
    tabs = []
    for g in (0, 1, NA_GROUPS - 1):
        strips = []
        for rq in range(NA_QROWS):
            off = _na_key_row0(g) - (g * NA_QROWS + rq) + (NA_KR - 1) + NA_KROWS
            strips.append(toe[:, :, off:off + NA_KROWS, :].reshape(N_NA_HEADS, GRID_W, NA_TK))
        vals = jnp.stack(strips, axis=1).reshape(N_NA_HEADS, NA_TQ, NA_TK) * LOG2E
        qi, kj = np.arange(NA_TQ), np.arange(NA_TK)
        r, c = g * NA_QROWS + qi // GRID_W, qi % GRID_W
        kr, kc = _na_key_row0(g) + kj // GRID_W, kj % GRID_W
        row_start = np.clip(r - NA_KR // 2, 0, GRID_ROWS - NA_KR)
        col_start = np.clip(c - NA_KC // 2, 0, GRID_W - NA_KC)
        valid = ((kr[None, :] >= row_start[:, None]) & (kr[None, :] < row_start[:, None] + NA_KR)
                 & (kc[None, :] >= col_start[:, None]) & (kc[None, :] < col_start[:, None] + NA_KC))
        tabs.append(jnp.where(jnp.asarray(valid)[None], vals, -jnp.inf))
    return jnp.stack(tabs, axis=1)


def _na_kernel(bias_ref, q_ref, k_ref, v_ref, o_ref, *s_refs):
    nt = (((1,), (1,)), ((), ()))
    for g in range(NA_GROUPS):
        keys = pl.ds(_na_key_row0(g) * GRID_W, NA_TK)
        s = lax.dot_general(q_ref[pl.ds(g * NA_TQ, NA_TQ), :], k_ref[keys, :], nt, preferred_element_type=F32)
        s_refs[g][...] = s + bias_ref[_na_table_id(g)]
    for g in range(NA_GROUPS):
        rows = pl.ds(g * NA_TQ, NA_TQ)
        keys = pl.ds(_na_key_row0(g) * GRID_W, NA_TK)
        s = s_refs[g][...]
        p = jnp.exp2(s - jnp.max(s, axis=-1, keepdims=True))
        r = 1.0 / jnp.sum(p, axis=-1, keepdims=True)
        o = jnp.dot(p.astype(BF16), v_ref[keys, :], preferred_element_type=F32) * r
        o_ref[rows, :] = o.astype(BF16)


def _na_attn(z3, bias_tabs):
    bsz = z3.shape[0]
    head_block = lambda blk: pl.BlockSpec((None, SEQ, HEAD_COLS), lambda h, b: (b, 0, blk + h))
    return pl.pallas_call(
        _na_kernel,
        out_shape=jax.ShapeDtypeStruct((bsz, SEQ, NA_WIDTH), BF16),
        grid=(N_NA_HEADS, bsz),
        in_specs=[
            pl.BlockSpec((None, 3, NA_TQ, NA_TK), lambda h, b: (h, 0, 0, 0)),
            head_block(QB_BLK), head_block(KB_BLK), head_block(VB_BLK),
        ],
        out_specs=head_block(0),
        scratch_shapes=[pltpu.VMEM((NA_TQ, NA_TK), F32)] * NA_GROUPS,
        compiler_params=_cparams(2, 32),
        name="na_attn",
    )(bias_tabs, z3, z3, z3)


def _layer_norm(r, g, b):
    mu = jnp.mean(r, axis=-1, keepdims=True)
    d = r - mu
    var = jnp.mean(d * d, axis=-1, keepdims=True)
    return d * lax.rsqrt(var + LN_EPS) * g + b


def _merge_kernel(oa_ref, ob_ref, ga_ref, gb_ref, x_ref, wa_ref, wb_ref, wo_ref, g_ref, b_ref,
                  x1_ref, x1b_ref):
    ya = jnp.dot(oa_ref[...], wa_ref[...], preferred_element_type=F32)
    yb = jnp.dot(ob_ref[...], wb_ref[...], preferred_element_type=F32)
    y = ga_ref[...].astype(F32) * ya + gb_ref[...].astype(F32) * yb
    hproj = jnp.dot(y.astype(BF16), wo_ref[...], preferred_element_type=F32)
    x1 = _layer_norm(DEEPNORM_ALPHA * x_ref[...] + hproj, g_ref[...], b_ref[...])
    x1_ref[...] = x1
    x1b_ref[...] = x1.astype(BF16)


def _merge(oa2d, ob2d, z2d, x2d, wa, wb, wo, ln_g, ln_b):
    m = x2d.shape[0]
    gate_blk = GATE_COL0 // D_MODEL
    const = lambda shape: pl.BlockSpec(shape, lambda i: (0, 0), pipeline_mode=pl.Buffered(1))
    return pl.pallas_call(
        _merge_kernel,
        out_shape=(jax.ShapeDtypeStruct((m, D_MODEL), F32), jax.ShapeDtypeStruct((m, D_MODEL), BF16)),
        grid=(m // MERGE_TM,),
        in_specs=[
            pl.BlockSpec((MERGE_TM, DIFF_WIDTH), lambda i: (i, 0)),
            pl.BlockSpec((MERGE_TM, NA_WIDTH), lambda i: (i, 0)),
            pl.BlockSpec((MERGE_TM, D_MODEL), lambda i: (i, gate_blk)),
            pl.BlockSpec((MERGE_TM, D_MODEL), lambda i: (i, gate_blk + 1)),
            pl.BlockSpec((MERGE_TM, D_MODEL), lambda i: (i, 0)),
            const((DIFF_WIDTH, D_MODEL)),
            const((NA_WIDTH, D_MODEL)),
            const((D_MODEL, D_MODEL)),
            const((1, D_MODEL)),
            const((1, D_MODEL)),
        ],
        out_specs=(pl.BlockSpec((MERGE_TM, D_MODEL), lambda i: (i, 0)),
                   pl.BlockSpec((MERGE_TM, D_MODEL), lambda i: (i, 0))),
        compiler_params=_cparams(1, 56),
        name="merge",
    )(oa2d, ob2d, z2d, z2d, x2d, wa, wb, wo, ln_g, ln_b)


def _seq_conv(hmat, cw_ref, cb_ref):
    n, t = hmat.shape[0], F32_SUBLANES
    edge_row = lax.broadcasted_iota(jnp.int32, (t, hmat.shape[1]), 0)
    prev = pltpu.roll(hmat, 1, 0)
    nxt = pltpu.roll(hmat, n - 1, 0)
    prev = jnp.concatenate([jnp.where(edge_row == 0, 0.0, prev[:t]), prev[t:]], axis=0)
    nxt = jnp.concatenate([nxt[:n - t], jnp.where(edge_row == t - 1, 0.0, nxt[n - t:])], axis=0)
    return cb_ref[...] + prev * cw_ref[0:1, :] + hmat * cw_ref[1:2, :] + nxt * cw_ref[2:3, :]


def _row_chunked_dot(x_ref, w_ref):
    return jnp.concatenate(
        [jnp.dot(x_ref[pl.ds(r0, FFN_TM), :], w_ref[...], preferred_element_type=F32)
         for r0 in range(0, x_ref.shape[0], FFN_TM)], axis=0)


def _ffn_up_kernel(x_ref, wg_ref, wv_ref, cwg_ref, cwv_ref, cbg_ref, cbv_ref, o_ref):
    gate = _seq_conv(_row_chunked_dot(x_ref, wg_ref), cwg_ref, cbg_ref)
    val = _seq_conv(_row_chunked_dot(x_ref, wv_ref), cwv_ref, cbv_ref)
    gelu = 0.5 * gate * (1.0 + lax.erf(gate * (2.0 ** -0.5)))
    o_ref[...] = (gelu * val).astype(BF16)


def _ffn_up(x1b, w_up, conv_w, conv_b):
    m = x1b.shape[0]
    return pl.pallas_call(
        _ffn_up_kernel,
        out_shape=jax.ShapeDtypeStruct((m, D_FF), BF16),
        grid=(m // SEQ, FFN_NJ),
        in_specs=[
            pl.BlockSpec((SEQ, D_MODEL), lambda b, j: (b, 0)),
            pl.BlockSpec((D_MODEL, FFN_TN), lambda b, j: (0, j)),
            pl.BlockSpec((D_MODEL, FFN_TN), lambda b, j: (0, FFN_NJ + j)),
            pl.BlockSpec((3, FFN_TN), lambda b, j: (0, j)),
            pl.BlockSpec((3, FFN_TN), lambda b, j: (0, FFN_NJ + j)),
            pl.BlockSpec((1, FFN_TN), lambda b, j: (0, j)),
            pl.BlockSpec((1, FFN_TN), lambda b, j: (0, FFN_NJ + j)),
        ],
        out_specs=pl.BlockSpec((SEQ, FFN_TN), lambda b, j: (b, j)),
        compiler_params=_cparams(2, 56),
        name="ffn_up",
    )(x1b, w_up, w_up, conv_w, conv_w, conv_b, conv_b)


def _ffn_down_kernel(g_ref, w_ref, x1_ref, lg_ref, lb_ref, o_ref):
    g = g_ref[...]
    f = jnp.concatenate(
        [jnp.dot(g, w_ref[:, pl.ds(c0, DOWN_SUB_N)], preferred_element_type=F32)
         for c0 in range(0, D_MODEL, DOWN_SUB_N)], axis=1)
    o_ref[...] = _layer_norm(DEEPNORM_ALPHA * x1_ref[...] + f, lg_ref[...], lb_ref[...])


def _ffn_down(g2d, w_down, x1, ln_g, ln_b):
    m = g2d.shape[0]
    const = lambda shape: pl.BlockSpec(shape, lambda i: (0, 0), pipeline_mode=pl.Buffered(1))
    return pl.pallas_call(
        _ffn_down_kernel,
        out_shape=jax.ShapeDtypeStruct((m, D_MODEL), F32),
        grid=(m // DOWN_TM,),
        in_specs=[
            pl.BlockSpec((DOWN_TM, D_FF), lambda i: (i, 0)),
            const((D_FF, D_MODEL)),
            pl.BlockSpec((DOWN_TM, D_MODEL), lambda i: (i, 0)),
            const((1, D_MODEL)),
            const((1, D_MODEL)),
        ],
        out_specs=pl.BlockSpec((DOWN_TM, D_MODEL), lambda i: (i, 0)),
        compiler_params=_cparams(1, 60),
        name="ffn_down",
    )(g2d, w_down, x1, ln_g, ln_b)


def _z_col_scale():
    s = np.ones((1, IN_COLS), np.float32)
    s[0, :DIFF_WIDTH] = DIFF_QK_DIM ** -0.5 * LOG2E
    s[0, 3 * DIFF_WIDTH:3 * DIFF_WIDTH + NA_WIDTH] = NA_HEAD_DIM ** -0.5 * LOG2E
    return s


def _alibi_slopes():
    n = N_DIFF_HEADS
    return np.array([2.0 ** (-8.0 * (i + 1) / n) for i in range(n)], dtype=np.float32)


def kernel(x, w_in, b_in, lam_q1, lam_k1, lam_q2, lam_k2, subln_g, rpb, w_branch_a, w_branch_b, w_out,
           ln1_g, ln1_b, w_up, conv_w, conv_b, w_down, ln2_g, ln2_b):
    bsz, seq, d = x.shape
    assert (seq, d) == (SEQ, D_MODEL) and w_in.shape[0] == 1
    row = lambda a: a.reshape(1, -1).astype(F32)
    x2d = x.reshape(bsz * seq, d)

    z = _in_proj(x2d, w_in[0].astype(BF16), row(b_in[0]), jnp.asarray(_z_col_scale()))
    z3 = z.reshape(bsz, seq, IN_COLS)
    oa = _diff_attn(z3, jnp.asarray(_alibi_slopes()), row(lam_q1[0]), row(lam_k1[0]), row(lam_q2[0]),
                    row(lam_k2[0]), row(subln_g[0]))
    ob = _na_attn(z3, _na_bias_tables(rpb[0]))
    x1, x1b = _merge(oa.reshape(bsz * seq, DIFF_WIDTH), ob.reshape(bsz * seq, NA_WIDTH), z, x2d,
                     w_branch_a[0].astype(BF16), w_branch_b[0].astype(BF16), w_out[0].astype(BF16),
                     row(ln1_g[0]), row(ln1_b[0]))
    g = _ffn_up(x1b, w_up[0].astype(BF16), conv_w[0].astype(F32), row(conv_b[0]))
    out = _ffn_down(g, w_down[0].astype(BF16), x1, row(ln2_g[0]), row(ln2_b[0]))
    return out.reshape(bsz, seq, d)
```

```python
import math
from functools import partial

import jax
import jax.numpy as jnp
import numpy as np
from jax import lax
from jax.experimental import pallas as pl
from jax.experimental.pallas import tpu as pltpu

F32 = jnp.float32
BF16 = jnp.bfloat16

D_MODEL = 2048
SEQ = 2048
GRID_W = 64
GRID_ROWS = SEQ // GRID_W
N_DIFF_HEADS = 8
DIFF_QK_DIM = 64
DIFF_V_DIM = 2 * DIFF_QK_DIM
DIFF_WIDTH = N_DIFF_HEADS * DIFF_V_DIM
N_NA_HEADS = 8
NA_HEAD_DIM = 128
NA_WIDTH = N_NA_HEADS * NA_HEAD_DIM
NA_KR = 8
NA_KC = 16
IN_COLS = 3 * DIFF_WIDTH + 3 * NA_WIDTH + 2 * D_MODEL
D_FF = 5632
LN_EPS = 1e-5
RMS_EPS = 1e-5
DEEPNORM_ALPHA = 2.0 ** 0.25
LAMBDA_INIT = 0.8 - 0.6 * math.exp(-0.3 * 0)
LOG2E = 1.4426950408889634

HEAD_COLS = 128
QA_BLK, KA_BLK, VA_BLK = 0, 8, 16
QB_BLK, KB_BLK, VB_BLK = 24, 32, 40
GATE_COL0 = 3 * DIFF_WIDTH + 3 * NA_WIDTH

MIB = 1024 * 1024
INPROJ_TM, INPROJ_TN = 1024, 2048
INPROJ_SUB_M = 512
DIFF_TQ = 1024
DIFF_SUB = 512
DIFF_ROWS = 16
NA_QROWS = 4
NA_KROWS = 12
NA_TQ = NA_QROWS * GRID_W
NA_TK = NA_KROWS * GRID_W
NA_GROUPS = GRID_ROWS // NA_QROWS
NA_BATCH = 2
MERGE_TM = 512
FFN_TN = 512
FFN_NJ = D_FF // FFN_TN
FFN_TM = 512
F32_SUBLANES = 8
DOWN_TM = 512
DOWN_SUB_N = 512


def _cparams(n_axes, vmem_mib):
    return pltpu.CompilerParams(
        dimension_semantics=("arbitrary",) * n_axes,
        vmem_limit_bytes=vmem_mib * MIB,
    )


def _inproj_kernel(x_ref, w_ref, b_ref, s_ref, z_ref, xb_ref):
    j = pl.program_id(1)

    @pl.when(j == 0)
    def _():
        xb_ref[...] = x_ref[...].astype(BF16)

    def project(finish):
        halves = range(0, INPROJ_TM, INPROJ_SUB_M)
        accs = [jnp.dot(xb_ref[pl.ds(r0, INPROJ_SUB_M), :], w_ref[...], preferred_element_type=F32)
                for r0 in halves]
        for r0, acc in zip(halves, accs):
            z_ref[pl.ds(r0, INPROJ_SUB_M), :] = finish((acc + b_ref[...]) * s_ref[...]).astype(BF16)

    gate0 = GATE_COL0 // INPROJ_TN

    @pl.when(j < gate0)
    def _():
        project(lambda acc: acc)

    @pl.when(j >= gate0)
    def _():
        project(lambda acc: 0.5 * jnp.tanh(0.5 * acc) + 0.5)


def _in_proj(x2d, w_bf, b_row, scale_row):
    m = x2d.shape[0]
    return pl.pallas_call(
        _inproj_kernel,
        out_shape=jax.ShapeDtypeStruct((m, IN_COLS), BF16),
        grid=(m // INPROJ_TM, IN_COLS // INPROJ_TN),
        in_specs=[
            pl.BlockSpec((INPROJ_TM, D_MODEL), lambda i, j: (i, 0)),
            pl.BlockSpec((D_MODEL, INPROJ_TN), lambda i, j: (0, j)),
            pl.BlockSpec((1, INPROJ_TN), lambda i, j: (0, j)),
            pl.BlockSpec((1, INPROJ_TN), lambda i, j: (0, j)),
        ],
        out_specs=pl.BlockSpec((INPROJ_TM, INPROJ_TN), lambda i, j: (i, j)),
        scratch_shapes=[pltpu.VMEM((INPROJ_TM, D_MODEL), BF16)],
        compiler_params=_cparams(2, 56),
        name="in_proj",
    )(x2d, w_bf, b_row, scale_row)


def _diff_attn_kernel(slopes_ref, lq1_ref, lk1_ref, lq2_ref, lk2_ref, g_ref,
                      q_ref, k_ref, v_ref, o_ref, bias_ref, *sub_refs):
    n_sub = DIFF_TQ // DIFF_SUB
    s_refs, a_refs = sub_refs[:2 * n_sub], sub_refs[2 * n_sub:]
    h = pl.program_id(0)
    qi = pl.program_id(1)
    b = pl.program_id(2)

    @pl.when(b == 0)
    def _():
        rows = lax.broadcasted_iota(jnp.int32, (DIFF_TQ, SEQ), 0) + qi * DIFF_TQ
        cols = lax.broadcasted_iota(jnp.int32, (DIFF_TQ, SEQ), 1)
        dist = jnp.abs(rows - cols).astype(F32)
        bias_ref[...] = dist * (-LOG2E * slopes_ref[h])

    lam = (jnp.exp(jnp.sum(lq1_ref[...] * lk1_ref[...], axis=-1, keepdims=True))
           - jnp.exp(jnp.sum(lq2_ref[...] * lk2_ref[...], axis=-1, keepdims=True))
           + LAMBDA_INIT)

    k = k_ref[...]
    nt = (((1,), (1,)), ((), ()))
    subs = range(DIFF_TQ // DIFF_SUB)
    for t in subs:
        q = q_ref[pl.ds(t * DIFF_SUB, DIFF_SUB), :]
        lane = lax.broadcasted_iota(jnp.int32, q.shape, 1)
        zero = jnp.zeros_like(q)
        bias = bias_ref[pl.ds(t * DIFF_SUB, DIFF_SUB), :]
        s_refs[2 * t][...] = lax.dot_general(jnp.where(lane < DIFF_QK_DIM, q, zero), k, nt,
                                             preferred_element_type=F32) + bias
        s_refs[2 * t + 1][...] = lax.dot_general(jnp.where(lane >= DIFF_QK_DIM, q, zero), k, nt,
                                                 preferred_element_type=F32) + bias
    for t in subs:
        s1_ref, s2_ref, a_ref = s_refs[2 * t], s_refs[2 * t + 1], a_refs[t]
        inv_l1 = []
        for r0 in range(0, DIFF_SUB, DIFF_ROWS):
            halves = []
            for r in range(r0, r0 + DIFF_ROWS, F32_SUBLANES):
                rows = pl.ds(r, F32_SUBLANES)
                x1 = s1_ref[rows, :]
                p1 = jnp.exp2(x1 - jnp.max(x1, axis=-1, keepdims=True))
                l1 = jnp.sum(p1, axis=-1, keepdims=True)
                x2 = s2_ref[rows, :]
                p2 = jnp.exp2(x2 - jnp.max(x2, axis=-1, keepdims=True))
                l2 = jnp.sum(p2, axis=-1, keepdims=True)
                halves.append(p1 - p2 * (lam * l1 / l2))
                inv_l1.append(1.0 / l1)
            a_ref[pl.ds(r0, DIFF_ROWS), :] = jnp.concatenate(halves, axis=0).astype(BF16)
        o = jnp.dot(a_ref[...], v_ref[...], preferred_element_type=F32) * jnp.concatenate(inv_l1, axis=0)
        ms = jnp.mean(o * o, axis=-1, keepdims=True)
        o = o * lax.rsqrt(ms + RMS_EPS) * g_ref[...] * (1.0 - LAMBDA_INIT)
        o_ref[pl.ds(t * DIFF_SUB, DIFF_SUB), :] = o.astype(BF16)


def _diff_attn(z3, slopes, lq1, lk1, lq2, lk2, subln_g):
    bsz = z3.shape[0]
    vec = lambda n: pl.BlockSpec((1, n), lambda h, qi, b: (0, 0))
    return pl.pallas_call(
        _diff_attn_kernel,
        out_shape=jax.ShapeDtypeStruct((bsz, SEQ, DIFF_WIDTH), BF16),
        grid=(N_DIFF_HEADS, SEQ // DIFF_TQ, bsz),
        in_specs=[
            pl.BlockSpec(memory_space=pltpu.SMEM),
            vec(DIFF_QK_DIM), vec(DIFF_QK_DIM), vec(DIFF_QK_DIM), vec(DIFF_QK_DIM),
            vec(DIFF_V_DIM),
            pl.BlockSpec((None, DIFF_TQ, HEAD_COLS), lambda h, qi, b: (b, qi, QA_BLK + h)),
            pl.BlockSpec((None, SEQ, HEAD_COLS), lambda h, qi, b: (b, 0, KA_BLK + h)),
            pl.BlockSpec((None, SEQ, HEAD_COLS), lambda h, qi, b: (b, 0, VA_BLK + h)),
        ],
        out_specs=pl.BlockSpec((None, DIFF_TQ, HEAD_COLS), lambda h, qi, b: (b, qi, h)),
        scratch_shapes=([pltpu.VMEM((DIFF_TQ, SEQ), F32)]
                        + [pltpu.VMEM((DIFF_SUB, SEQ), F32)] * (2 * (DIFF_TQ // DIFF_SUB))
                        + [pltpu.VMEM((DIFF_SUB, SEQ), BF16)] * (DIFF_TQ // DIFF_SUB)),
        compiler_params=_cparams(3, 48),
        name="diff_attn",
    )(slopes, lq1, lk1, lq2, lk2, subln_g, z3, z3, z3)


def _na_key_row0(g):
    return min(max(g * NA_QROWS - NA_KR // 2, 0), GRID_ROWS - NA_KROWS)


def _na_table_id(g):
    return 0 if g == 0 else (2 if g == NA_GROUPS - 1 else 1)


def _na_bias_tables(rpb):
    n_dr, n_dc = 2 * NA_KR - 1, 2 * NA_KC - 1
    period = 2 * GRID_W
    w = jnp.pad(rpb.astype(F32), ((0, 0), (0, 0), (0, period - n_dc)))
    f = jnp.tile(w, (1, 1, GRID_W))[..., :GRID_W * (period - 1)]
    toe = f.reshape(N_NA_HEADS, n_dr, GRID_W, period - 1)[..., NA_KC - 1:NA_KC - 1 + GRID_W]
    toe = jnp.pad(toe.transpose(0, 2, 1, 3), ((0, 0), (0, 0), (NA_KROWS, NA_KROWS), (0, 0)))
    tabs = []
    for g in (0, 1, NA_GROUPS - 1):
        strips = []
        for rq in range(NA_QROWS):
            off = _na_key_row0(g) - (g * NA_QROWS + rq) + (NA_KR - 1) + NA_KROWS
            strips.append(toe[:, :, off:off + NA_KROWS, :].reshape(N_NA_HEADS, GRID_W, NA_TK))
        vals = jnp.stack(strips, axis=1).reshape(N_NA_HEADS, NA_TQ, NA_TK) * LOG2E
        qi, kj = np.arange(NA_TQ), np.arange(NA_TK)
        r, c = g * NA_QROWS + qi // GRID_W, qi % GRID_W
        kr, kc = _na_key_row0(g) + kj // GRID_W, kj % GRID_W
        row_start = np.clip(r - NA_KR // 2, 0, GRID_ROWS - NA_KR)
        col_start = np.clip(c - NA_KC // 2, 0, GRID_W - NA_KC)
        valid = ((kr[None, :] >= row_start[:, None]) & (kr[None, :] < row_start[:, None] + NA_KR)
                 & (kc[None, :] >= col_start[:, None]) & (kc[None, :] < col_start[:, None] + NA_KC))
        tabs.append(jnp.where(jnp.asarray(valid)[None], vals, -jnp.inf))
    return jnp.stack(tabs, axis=1)


def _na_kernel(bias_ref, q_ref, k_ref, v_ref, o_ref, *s_refs):
    nt = (((1,), (1,)), ((), ()))
    units = [(bb, g) for bb in range(NA_BATCH) for g in range(NA_GROUPS)]
    for u, (bb, g) in enumerate(units):
        keys = pl.ds(_na_key_row0(g) * GRID_W, NA_TK)
        s = lax.dot_general(q_ref[bb, pl.ds(g * NA_TQ, NA_TQ), :], k_ref[bb, keys, :], nt,
                            preferred_element_type=F32)
        s_refs[u][...] = s + bias_ref[_na_table_id(g)]
    for u, (bb, g) in enumerate(units):
        rows = pl.ds(g * NA_TQ, NA_TQ)
        keys = pl.ds(_na_key_row0(g) * GRID_W, NA_TK)
        s = s_refs[u][...]
        p = jnp.exp2(s - jnp.max(s, axis=-1, keepdims=True))
        r = 1.0 / jnp.sum(p, axis=-1, keepdims=True)
        o = jnp.dot(p.astype(BF16), v_ref[bb, keys, :], preferred_element_type=F32) * r
        o_ref[bb, rows, :] = o.astype(BF16)


def _na_attn(z3, bias_tabs):
    bsz = z3.shape[0]
    head_block = lambda blk: pl.BlockSpec((NA_BATCH, SEQ, HEAD_COLS), lambda h, b: (b, 0, blk + h))
    return pl.pallas_call(
        _na_kernel,
        out_shape=jax.ShapeDtypeStruct((bsz, SEQ, NA_WIDTH), BF16),
        grid=(N_NA_HEADS, bsz // NA_BATCH),
        in_specs=[
            pl.BlockSpec((None, 3, NA_TQ, NA_TK), lambda h, b: (h, 0, 0, 0)),
            head_block(QB_BLK), head_block(KB_BLK), head_block(VB_BLK),
        ],
        out_specs=head_block(0),
        scratch_shapes=[pltpu.VMEM((NA_TQ, NA_TK), F32)] * (NA_BATCH * NA_GROUPS),
        compiler_params=_cparams(2, 40),
        name="na_attn",
    )(bias_tabs, z3, z3, z3)


def _layer_norm(r, g, b):
    mu = jnp.mean(r, axis=-1, keepdims=True)
    d = r - mu
    var = jnp.mean(d * d, axis=-1, keepdims=True)
    return d * lax.rsqrt(var + LN_EPS) * g + b


def _merge_kernel(oa_ref, ob_ref, ga_ref, gb_ref, x_ref, wa_ref, wb_ref, wo_ref, g_ref, b_ref,
                  x1_ref, x1b_ref):
    ya = jnp.dot(oa_ref[...], wa_ref[...], preferred_element_type=F32)
    yb = jnp.dot(ob_ref[...], wb_ref[...], preferred_element_type=F32)
    y = ga_ref[...].astype(F32) * ya + gb_ref[...].astype(F32) * yb
    hproj = jnp.dot(y.astype(BF16), wo_ref[...], preferred_element_type=F32)
    x1 = _layer_norm(DEEPNORM_ALPHA * x_ref[...] + hproj, g_ref[...], b_ref[...])
    x1_ref[...] = x1
    x1b_ref[...] = x1.astype(BF16)


def _merge(oa2d, ob2d, z2d, x2d, wa, wb, wo, ln_g, ln_b):
    m = x2d.shape[0]
    gate_blk = GATE_COL0 // D_MODEL
    const = lambda shape: pl.BlockSpec(shape, lambda i: (0, 0), pipeline_mode=pl.Buffered(1))
    return pl.pallas_call(
        _merge_kernel,
        out_shape=(jax.ShapeDtypeStruct((m, D_MODEL), F32), jax.ShapeDtypeStruct((m, D_MODEL), BF16)),
        grid=(m // MERGE_TM,),
        in_specs=[
            pl.BlockSpec((MERGE_TM, DIFF_WIDTH), lambda i: (i, 0)),
            pl.BlockSpec((MERGE_TM, NA_WIDTH), lambda i: (i, 0)),
            pl.BlockSpec((MERGE_TM, D_MODEL), lambda i: (i, gate_blk)),
            pl.BlockSpec((MERGE_TM, D_MODEL), lambda i: (i, gate_blk + 1)),
            pl.BlockSpec((MERGE_TM, D_MODEL), lambda i: (i, 0)),
            const((DIFF_WIDTH, D_MODEL)),
            const((NA_WIDTH, D_MODEL)),
            const((D_MODEL, D_MODEL)),
            const((1, D_MODEL)),
            const((1, D_MODEL)),
        ],
        out_specs=(pl.BlockSpec((MERGE_TM, D_MODEL), lambda i: (i, 0)),
                   pl.BlockSpec((MERGE_TM, D_MODEL), lambda i: (i, 0))),
        compiler_params=_cparams(1, 56),
        name="merge",
    )(oa2d, ob2d, z2d, z2d, x2d, wa, wb, wo, ln_g, ln_b)


def _seq_conv(hmat, cw_ref, cb_ref):
    n, t = hmat.shape[0], F32_SUBLANES
    edge_row = lax.broadcasted_iota(jnp.int32, (t, hmat.shape[1]), 0)
    prev = pltpu.roll(hmat, 1, 0)
    nxt = pltpu.roll(hmat, n - 1, 0)
    prev = jnp.concatenate([jnp.where(edge_row == 0, 0.0, prev[:t]), prev[t:]], axis=0)
    nxt = jnp.concatenate([nxt[:n - t], jnp.where(edge_row == t - 1, 0.0, nxt[n - t:])], axis=0)
    return cb_ref[...] + prev * cw_ref[0:1, :] + hmat * cw_ref[1:2, :] + nxt * cw_ref[2:3, :]


def _row_chunked_dot(x_ref, w_ref):
    return jnp.concatenate(
        [jnp.dot(x_ref[pl.ds(r0, FFN_TM), :], w_ref[...], preferred_element_type=F32)
         for r0 in range(0, x_ref.shape[0], FFN_TM)], axis=0)


def _ffn_up_kernel(x_ref, wg_ref, wv_ref, cwg_ref, cwv_ref, cbg_ref, cbv_ref, o_ref):
    gate = _seq_conv(_row_chunked_dot(x_ref, wg_ref), cwg_ref, cbg_ref)
    val = _seq_conv(_row_chunked_dot(x_ref, wv_ref), cwv_ref, cbv_ref)
    gelu = 0.5 * gate * (1.0 + lax.erf(gate * (2.0 ** -0.5)))
    o_ref[...] = (gelu * val).astype(BF16)


def _ffn_up(x1b, w_up, conv_w, conv_b):
    m = x1b.shape[0]
    return pl.pallas_call(
        _ffn_up_kernel,
        out_shape=jax.ShapeDtypeStruct((m, D_FF), BF16),
        grid=(m // SEQ, FFN_NJ),
        in_specs=[
            pl.BlockSpec((SEQ, D_MODEL), lambda b, j: (b, 0)),
            pl.BlockSpec((D_MODEL, FFN_TN), lambda b, j: (0, j)),
            pl.BlockSpec((D_MODEL, FFN_TN), lambda b, j: (0, FFN_NJ + j)),
            pl.BlockSpec((3, FFN_TN), lambda b, j: (0, j)),
            pl.BlockSpec((3, FFN_TN), lambda b, j: (0, FFN_NJ + j)),
            pl.BlockSpec((1, FFN_TN), lambda b, j: (0, j)),
            pl.BlockSpec((1, FFN_TN), lambda b, j: (0, FFN_NJ + j)),
        ],
        out_specs=pl.BlockSpec((SEQ, FFN_TN), lambda b, j: (b, j)),
        compiler_params=_cparams(2, 56),
        name="ffn_up",
    )(x1b, w_up, w_up, conv_w, conv_w, conv_b, conv_b)


def _ffn_down_kernel(g_ref, w_ref, x1_ref, lg_ref, lb_ref, o_ref):
    g = g_ref[...]
    f = jnp.concatenate(
        [jnp.dot(g, w_ref[:, pl.ds(c0, DOWN_SUB_N)], preferred_element_type=F32)
         for c0 in range(0, D_MODEL, DOWN_SUB_N)], axis=1)
    o_ref[...] = _layer_norm(DEEPNORM_ALPHA * x1_ref[...] + f, lg_ref[...], lb_ref[...])


def _ffn_down(g2d, w_down, x1, ln_g, ln_b):
    m = g2d.shape[0]
    const = lambda shape: pl.BlockSpec(shape, lambda i: (0, 0), pipeline_mode=pl.Buffered(1))
    return pl.pallas_call(
        _ffn_down_kernel,
        out_shape=jax.ShapeDtypeStruct((m, D_MODEL), F32),
        grid=(m // DOWN_TM,),
        in_specs=[
            pl.BlockSpec((DOWN_TM, D_FF), lambda i: (i, 0)),
            const((D_FF, D_MODEL)),
            pl.BlockSpec((DOWN_TM, D_MODEL), lambda i: (i, 0)),
            const((1, D_MODEL)),
            const((1, D_MODEL)),
        ],
        out_specs=pl.BlockSpec((DOWN_TM, D_MODEL), lambda i: (i, 0)),
        compiler_params=_cparams(1, 60),
        name="ffn_down",
    )(g2d, w_down, x1, ln_g, ln_b)


def _z_col_scale():
    s = np.ones((1, IN_COLS), np.float32)
    s[0, :DIFF_WIDTH] = DIFF_QK_DIM ** -0.5 * LOG2E
    s[0, 3 * DIFF_WIDTH:3 * DIFF_WIDTH + NA_WIDTH] = NA_HEAD_DIM ** -0.5 * LOG2E
    return s


def _alibi_slopes():
    n = N_DIFF_HEADS
    return np.array([2.0 ** (-8.0 * (i + 1) / n) for i in range(n)], dtype=np.float32)


def kernel(x, w_in, b_in, lam_q1, lam_k1, lam_q2, lam_k2, subln_g, rpb, w_branch_a, w_branch_b, w_out,
           ln1_g, ln1_b, w_up, conv_w, conv_b, w_down, ln2_g, ln2_b):
    bsz, seq, d = x.shape
    assert (seq, d) == (SEQ, D_MODEL) and w_in.shape[0] == 1
    row = lambda a: a.reshape(1, -1).astype(F32)
    x2d = x.reshape(bsz * seq, d)

    z = _in_proj(x2d, w_in[0].astype(BF16), row(b_in[0]), jnp.asarray(_z_col_scale()))
    z3 = z.reshape(bsz, seq, IN_COLS)
    oa = _diff_attn(z3, jnp.asarray(_alibi_slopes()), row(lam_q1[0]), row(lam_k1[0]), row(lam_q2[0]),
                    row(lam_k2[0]), row(subln_g[0]))
    ob = _na_attn(z3, _na_bias_tables(rpb[0]))
    x1, x1b = _merge(oa.reshape(bsz * seq, DIFF_WIDTH), ob.reshape(bsz * seq, NA_WIDTH), z, x2d,
                     w_branch_a[0].astype(BF16), w_branch_b[0].astype(BF16), w_out[0].astype(BF16),
                     row(ln1_g[0]), row(ln1_b[0]))
    g = _ffn_up(x1b, w_up[0].astype(BF16), conv_w[0].astype(F32), row(conv_b[0]))
    out = _ffn_down(g, w_down[0].astype(BF16), x1, row(ln2_g[0]), row(ln2_b[0]))
    return out.reshape(bsz, seq, d)
```

```python
import math
from functools import partial

import jax
import jax.numpy as jnp
import numpy as np
from jax import lax
from jax.experimental import pallas as pl
from jax.experimental.pallas import tpu as pltpu

F32 = jnp.float32
BF16 = jnp.bfloat16

D_MODEL = 2048
SEQ = 2048
GRID_W = 64
GRID_ROWS = SEQ // GRID_W
N_DIFF_HEADS = 8
DIFF_QK_DIM = 64
DIFF_V_DIM = 2 * DIFF_QK_DIM
DIFF_WIDTH = N_DIFF_HEADS * DIFF_V_DIM
N_NA_HEADS = 8
NA_HEAD_DIM = 128
NA_WIDTH = N_NA_HEADS * NA_HEAD_DIM
NA_KR = 8
NA_KC = 16
IN_COLS = 3 * DIFF_WIDTH + 3 * NA_WIDTH + 2 * D_MODEL
D_FF = 5632
LN_EPS = 1e-5
RMS_EPS = 1e-5
DEEPNORM_ALPHA = 2.0 ** 0.25
LAMBDA_INIT = 0.8 - 0.6 * math.exp(-0.3 * 0)
LOG2E = 1.4426950408889634

HEAD_COLS = 128
QA_BLK, KA_BLK, VA_BLK = 0, 8, 16
QB_BLK, KB_BLK, VB_BLK = 24, 32, 40
GATE_COL0 = 3 * DIFF_WIDTH + 3 * NA_WIDTH

MIB = 1024 * 1024
INPROJ_TM, INPROJ_TN = 1024, 2048
INPROJ_SUB_M = 512
DIFF_TQ = 1024
DIFF_SUB = 512
DIFF_ROWS = 16
NA_QROWS = 4
NA_KROWS = 12
NA_TQ = NA_QROWS * GRID_W
NA_TK = NA_KROWS * GRID_W
NA_GROUPS = GRID_ROWS // NA_QROWS
MERGE_TM = 512
FFN_TN = 512
FFN_NJ = D_FF // FFN_TN
FFN_TM = 256
F32_SUBLANES = 8
DOWN_TM = 512
DOWN_SUB_N = 512


def _cparams(n_axes, vmem_mib):
    return pltpu.CompilerParams(
        dimension_semantics=("arbitrary",) * n_axes,
        vmem_limit_bytes=vmem_mib * MIB,
    )


def _inproj_kernel(x_ref, w_ref, b_ref, s_ref, z_ref, xb_ref):
    j = pl.program_id(1)

    @pl.when(j == 0)
    def _():
        xb_ref[...] = x_ref[...].astype(BF16)

    def project(finish):
        halves = range(0, INPROJ_TM, INPROJ_SUB_M)
        accs = [jnp.dot(xb_ref[pl.ds(r0, INPROJ_SUB_M), :], w_ref[...], preferred_element_type=F32)
                for r0 in halves]
        for r0, acc in zip(halves, accs):
            z_ref[pl.ds(r0, INPROJ_SUB_M), :] = finish((acc + b_ref[...]) * s_ref[...]).astype(BF16)

    gate0 = GATE_COL0 // INPROJ_TN

    @pl.when(j < gate0)
    def _():
        project(lambda acc: acc)

    @pl.when(j >= gate0)
    def _():
        project(lambda acc: 0.5 * jnp.tanh(0.5 * acc) + 0.5)


def _in_proj(x2d, w_bf, b_row, scale_row):
    m = x2d.shape[0]
    return pl.pallas_call(
        _inproj_kernel,
        out_shape=jax.ShapeDtypeStruct((m, IN_COLS), BF16),
        grid=(m // INPROJ_TM, IN_COLS // INPROJ_TN),
        in_specs=[
            pl.BlockSpec((INPROJ_TM, D_MODEL), lambda i, j: (i, 0)),
            pl.BlockSpec((D_MODEL, INPROJ_TN), lambda i, j: (0, j)),
            pl.BlockSpec((1, INPROJ_TN), lambda i, j: (0, j)),
            pl.BlockSpec((1, INPROJ_TN), lambda i, j: (0, j)),
        ],
        out_specs=pl.BlockSpec((INPROJ_TM, INPROJ_TN), lambda i, j: (i, j)),
        scratch_shapes=[pltpu.VMEM((INPROJ_TM, D_MODEL), BF16)],
        compiler_params=_cparams(2, 56),
        name="in_proj",
    )(x2d, w_bf, b_row, scale_row)


def _diff_attn_kernel(slopes_ref, lq1_ref, lk1_ref, lq2_ref, lk2_ref, g_ref,
                      q_ref, k_ref, v_ref, o_ref, bias_ref, *sub_refs):
    n_sub = DIFF_TQ // DIFF_SUB
    s_refs, a_refs = sub_refs[:2 * n_sub], sub_refs[2 * n_sub:]
    h = pl.program_id(0)
    qi = pl.program_id(1)
    b = pl.program_id(2)

    @pl.when(b == 0)
    def _():
        rows = lax.broadcasted_iota(jnp.int32, (DIFF_TQ, SEQ), 0) + qi * DIFF_TQ
        cols = lax.broadcasted_iota(jnp.int32, (DIFF_TQ, SEQ), 1)
        dist = jnp.abs(rows - cols).astype(F32)
        bias_ref[...] = dist * (-LOG2E * slopes_ref[h])

    lam = (jnp.exp(jnp.sum(lq1_ref[...] * lk1_ref[...], axis=-1, keepdims=True))
           - jnp.exp(jnp.sum(lq2_ref[...] * lk2_ref[...], axis=-1, keepdims=True))
           + LAMBDA_INIT)

    k = k_ref[...]
    nt = (((1,), (1,)), ((), ()))
    subs = range(DIFF_TQ // DIFF_SUB)
    for t in subs:
        q = q_ref[pl.ds(t * DIFF_SUB, DIFF_SUB), :]
        lane = lax.broadcasted_iota(jnp.int32, q.shape, 1)
        zero = jnp.zeros_like(q)
        bias = bias_ref[pl.ds(t * DIFF_SUB, DIFF_SUB), :]
        s_refs[2 * t][...] = lax.dot_general(jnp.where(lane < DIFF_QK_DIM, q, zero), k, nt,
                                             preferred_element_type=F32) + bias
        s_refs[2 * t + 1][...] = lax.dot_general(jnp.where(lane >= DIFF_QK_DIM, q, zero), k, nt,
                                                 preferred_element_type=F32) + bias
    for t in subs:
        s1_ref, s2_ref, a_ref = s_refs[2 * t], s_refs[2 * t + 1], a_refs[t]
        inv_l1 = []
        for r0 in range(0, DIFF_SUB, DIFF_ROWS):
            halves = []
            for r in range(r0, r0 + DIFF_ROWS, F32_SUBLANES):
                rows = pl.ds(r, F32_SUBLANES)
                x1 = s1_ref[rows, :]
                p1 = jnp.exp2(x1 - jnp.max(x1, axis=-1, keepdims=True))
                l1 = jnp.sum(p1, axis=-1, keepdims=True)
                x2 = s2_ref[rows, :]
                p2 = jnp.exp2(x2 - jnp.max(x2, axis=-1, keepdims=True))
                l2 = jnp.sum(p2, axis=-1, keepdims=True)
                halves.append(p1 - p2 * (lam * l1 / l2))
                inv_l1.append(1.0 / l1)
            a_ref[pl.ds(r0, DIFF_ROWS), :] = jnp.concatenate(halves, axis=0).astype(BF16)
        o = jnp.dot(a_ref[...], v_ref[...], preferred_element_type=F32) * jnp.concatenate(inv_l1, axis=0)
        ms = jnp.mean(o * o, axis=-1, keepdims=True)
        o = o * lax.rsqrt(ms + RMS_EPS) * g_ref[...] * (1.0 - LAMBDA_INIT)
        o_ref[pl.ds(t * DIFF_SUB, DIFF_SUB), :] = o.astype(BF16)


def _diff_attn(z3, slopes, lq1, lk1, lq2, lk2, subln_g):
    bsz = z3.shape[0]
    vec = lambda n: pl.BlockSpec((1, n), lambda h, qi, b: (0, 0))
    return pl.pallas_call(
        _diff_attn_kernel,
        out_shape=jax.ShapeDtypeStruct((bsz, SEQ, DIFF_WIDTH), BF16),
        grid=(N_DIFF_HEADS, SEQ // DIFF_TQ, bsz),
        in_specs=[
            pl.BlockSpec(memory_space=pltpu.SMEM),
            vec(DIFF_QK_DIM), vec(DIFF_QK_DIM), vec(DIFF_QK_DIM), vec(DIFF_QK_DIM),
            vec(DIFF_V_DIM),
            pl.BlockSpec((None, DIFF_TQ, HEAD_COLS), lambda h, qi, b: (b, qi, QA_BLK + h)),
            pl.BlockSpec((None, SEQ, HEAD_COLS), lambda h, qi, b: (b, 0, KA_BLK + h)),
            pl.BlockSpec((None, SEQ, HEAD_COLS), lambda h, qi, b: (b, 0, VA_BLK + h)),
        ],
        out_specs=pl.BlockSpec((None, DIFF_TQ, HEAD_COLS), lambda h, qi, b: (b, qi, h)),
        scratch_shapes=([pltpu.VMEM((DIFF_TQ, SEQ), F32)]
                        + [pltpu.VMEM((DIFF_SUB, SEQ), F32)] * (2 * (DIFF_TQ // DIFF_SUB))
                        + [pltpu.VMEM((DIFF_SUB, SEQ), BF16)] * (DIFF_TQ // DIFF_SUB)),
        compiler_params=_cparams(3, 48),
        name="diff_attn",
    )(slopes, lq1, lk1, lq2, lk2, subln_g, z3, z3, z3)


def _na_key_row0(g):
    return min(max(g * NA_QROWS - NA_KR // 2, 0), GRID_ROWS - NA_KROWS)


def _na_table_id(g):
    return 0 if g == 0 else (2 if g == NA_GROUPS - 1 else 1)


def _na_bias_tables(rpb):
    n_dr, n_dc = 2 * NA_KR - 1, 2 * NA_KC - 1
    period = 2 * GRID_W
    w = jnp.pad(rpb.astype(F32), ((0, 0), (0, 0), (0, period - n_dc)))
    f = jnp.tile(w, (1, 1, GRID_W))[..., :GRID_W * (period - 1)]
    toe = f.reshape(N_NA_HEADS, n_dr, GRID_W, period - 1)[..., NA_KC - 1:NA_KC - 1 + GRID_W]
    toe = jnp.pad(toe.transpose(0, 2, 1, 3), ((0, 0), (0, 0), (NA_KROWS, NA_KROWS), (0, 0)))
    tabs = []
    for g in (0, 1, NA_GROUPS - 1):
        strips = []
        for rq in range(NA_QROWS):
            off = _na_key_row0(g) - (g * NA_QROWS + rq) + (NA_KR - 1) + NA_KROWS
            strips.append(toe[:, :, off:off + NA_KROWS, :].reshape(N_NA_HEADS, GRID_W, NA_TK))
        vals = jnp.stack(strips, axis=1).reshape(N_NA_HEADS, NA_TQ, NA_TK) * LOG2E
        qi, kj = np.arange(NA_TQ), np.arange(NA_TK)
        r, c = g * NA_QROWS + qi // GRID_W, qi % GRID_W
        kr, kc = _na_key_row0(g) + kj // GRID_W, kj % GRID_W
        row_start = np.clip(r - NA_KR // 2, 0, GRID_ROWS - NA_KR)
        col_start = np.clip(c - NA_KC // 2, 0, GRID_W - NA_KC)
        valid = ((kr[None, :] >= row_start[:, None]) & (kr[None, :] < row_start[:, None] + NA_KR)
                 & (kc[None, :] >= col_start[:, None]) & (kc[None, :] < col_start[:, None] + NA_KC))
        tabs.append(jnp.where(jnp.asarray(valid)[None], vals, -jnp.inf))
    return jnp.stack(tabs, axis=1)


def _na_kernel(bias_ref, q_ref, k_ref, v_ref, o_ref, *s_refs):
    nt = (((1,), (1,)), ((), ()))
    for g in range(NA_GROUPS):
        keys = pl.ds(_na_key_row0(g) * GRID_W, NA_TK)
        s = lax.dot_general(q_ref[pl.ds(g * NA_TQ, NA_TQ), :], k_ref[keys, :], nt, preferred_element_type=F32)
        s_refs[g][...] = s + bias_ref[_na_table_id(g)]
    for g in range(NA_GROUPS):
        rows = pl.ds(g * NA_TQ, NA_TQ)
        keys = pl.ds(_na_key_row0(g) * GRID_W, NA_TK)
        s = s_refs[g][...]
        p = jnp.exp2(s - jnp.max(s, axis=-1, keepdims=True))
        r = 1.0 / jnp.sum(p, axis=-1, keepdims=True)
        o = jnp.dot(p.astype(BF16), v_ref[keys, :], preferred_element_type=F32) * r
        o_ref[rows, :] = o.astype(BF16)


def _na_attn(z3, bias_tabs):
    bsz = z3.shape[0]
    head_block = lambda blk: pl.BlockSpec((None, SEQ, HEAD_COLS), lambda h, b: (b, 0, blk + h))
    return pl.pallas_call(
        _na_kernel,
        out_shape=jax.ShapeDtypeStruct((bsz, SEQ, NA_WIDTH), BF16),
        grid=(N_NA_HEADS, bsz),
        in_specs=[
            pl.BlockSpec((None, 3, NA_TQ, NA_TK), lambda h, b: (h, 0, 0, 0)),
            head_block(QB_BLK), head_block(KB_BLK), head_block(VB_BLK),
        ],
        out_specs=head_block(0),
        scratch_shapes=[pltpu.VMEM((NA_TQ, NA_TK), F32)] * NA_GROUPS,
        compiler_params=_cparams(2, 32),
        name="na_attn",
    )(bias_tabs, z3, z3, z3)


def _layer_norm(r, g, b):
    mu = jnp.mean(r, axis=-1, keepdims=True)
    d = r - mu
    var = jnp.mean(d * d, axis=-1, keepdims=True)
    return d * lax.rsqrt(var + LN_EPS) * g + b


def _merge_kernel(oa_ref, ob_ref, ga_ref, gb_ref, x_ref, wa_ref, wb_ref, wo_ref, g_ref, b_ref,
                  x1_ref, x1b_ref):
    ya = jnp.dot(oa_ref[...], wa_ref[...], preferred_element_type=F32)
    yb = jnp.dot(ob_ref[...], wb_ref[...], preferred_element_type=F32)
    y = ga_ref[...].astype(F32) * ya + gb_ref[...].astype(F32) * yb
    hproj = jnp.dot(y.astype(BF16), wo_ref[...], preferred_element_type=F32)
    x1 = _layer_norm(DEEPNORM_ALPHA * x_ref[...] + hproj, g_ref[...], b_ref[...])
    x1_ref[...] = x1
    x1b_ref[...] = x1.astype(BF16)


def _merge(oa2d, ob2d, z2d, x2d, wa, wb, wo, ln_g, ln_b):
    m = x2d.shape[0]
    gate_blk = GATE_COL0 // D_MODEL
    const = lambda shape: pl.BlockSpec(shape, lambda i: (0, 0), pipeline_mode=pl.Buffered(1))
    return pl.pallas_call(
        _merge_kernel,
        out_shape=(jax.ShapeDtypeStruct((m, D_MODEL), F32), jax.ShapeDtypeStruct((m, D_MODEL), BF16)),
        grid=(m // MERGE_TM,),
        in_specs=[
            pl.BlockSpec((MERGE_TM, DIFF_WIDTH), lambda i: (i, 0)),
            pl.BlockSpec((MERGE_TM, NA_WIDTH), lambda i: (i, 0)),
            pl.BlockSpec((MERGE_TM, D_MODEL), lambda i: (i, gate_blk)),
            pl.BlockSpec((MERGE_TM, D_MODEL), lambda i: (i, gate_blk + 1)),
            pl.BlockSpec((MERGE_TM, D_MODEL), lambda i: (i, 0)),
            const((DIFF_WIDTH, D_MODEL)),
            const((NA_WIDTH, D_MODEL)),
            const((D_MODEL, D_MODEL)),
            const((1, D_MODEL)),
            const((1, D_MODEL)),
        ],
        out_specs=(pl.BlockSpec((MERGE_TM, D_MODEL), lambda i: (i, 0)),
                   pl.BlockSpec((MERGE_TM, D_MODEL), lambda i: (i, 0))),
        compiler_params=_cparams(1, 56),
        name="merge",
    )(oa2d, ob2d, z2d, z2d, x2d, wa, wb, wo, ln_g, ln_b)


def _seq_conv(hmat, cw_ref, cb_ref):
    n, t = hmat.shape[0], F32_SUBLANES
    edge_row = lax.broadcasted_iota(jnp.int32, (t, hmat.shape[1]), 0)
    prev = pltpu.roll(hmat, 1, 0)
    nxt = pltpu.roll(hmat, n - 1, 0)
    prev = jnp.concatenate([jnp.where(edge_row == 0, 0.0, prev[:t]), prev[t:]], axis=0)
    nxt = jnp.concatenate([nxt[:n - t], jnp.where(edge_row == t - 1, 0.0, nxt[n - t:])], axis=0)
    return cb_ref[...] + prev * cw_ref[0:1, :] + hmat * cw_ref[1:2, :] + nxt * cw_ref[2:3, :]


def _row_chunked_dot(x_ref, w_ref):
    return jnp.concatenate(
        [jnp.dot(x_ref[pl.ds(r0, FFN_TM), :], w_ref[...], preferred_element_type=F32)
         for r0 in range(0, x_ref.shape[0], FFN_TM)], axis=0)


def _ffn_up_kernel(x_ref, wg_ref, wv_ref, cwg_ref, cwv_ref, cbg_ref, cbv_ref, o_ref):
    gate = _seq_conv(_row_chunked_dot(x_ref, wg_ref), cwg_ref, cbg_ref)
    val = _seq_conv(_row_chunked_dot(x_ref, wv_ref), cwv_ref, cbv_ref)
    gelu = 0.5 * gate * (1.0 + lax.erf(gate * (2.0 ** -0.5)))
    o_ref[...] = (gelu * val).astype(BF16)


def _ffn_up(x1b, w_up, conv_w, conv_b):
    m = x1b.shape[0]
    return pl.pallas_call(
        _ffn_up_kernel,
        out_shape=jax.ShapeDtypeStruct((m, D_FF), BF16),
        grid=(m // SEQ, FFN_NJ),
        in_specs=[
            pl.BlockSpec((SEQ, D_MODEL), lambda b, j: (b, 0)),
            pl.BlockSpec((D_MODEL, FFN_TN), lambda b, j: (0, j)),
            pl.BlockSpec((D_MODEL, FFN_TN), lambda b, j: (0, FFN_NJ + j)),
            pl.BlockSpec((3, FFN_TN), lambda b, j: (0, j)),
            pl.BlockSpec((3, FFN_TN), lambda b, j: (0, FFN_NJ + j)),
            pl.BlockSpec((1, FFN_TN), lambda b, j: (0, j)),
            pl.BlockSpec((1, FFN_TN), lambda b, j: (0, FFN_NJ + j)),
        ],
        out_specs=pl.BlockSpec((SEQ, FFN_TN), lambda b, j: (b, j)),
        compiler_params=_cparams(2, 56),
        name="ffn_up",
    )(x1b, w_up, w_up, conv_w, conv_w, conv_b, conv_b)


def _ffn_down_kernel(g_ref, w_ref, x1_ref, lg_ref, lb_ref, o_ref):
    g = g_ref[...]
    f = jnp.concatenate(
        [jnp.dot(g, w_ref[:, pl.ds(c0, DOWN_SUB_N)], preferred_element_type=F32)
         for c0 in range(0, D_MODEL, DOWN_SUB_N)], axis=1)
    o_ref[...] = _layer_norm(DEEPNORM_ALPHA * x1_ref[...] + f, lg_ref[...], lb_ref[...])


def _ffn_down(g2d, w_down, x1, ln_g, ln_b):
    m = g2d.shape[0]
    const = lambda shape: pl.BlockSpec(shape, lambda i: (0, 0), pipeline_mode=pl.Buffered(1))
    return pl.pallas_call(
        _ffn_down_kernel,
        out_shape=jax.ShapeDtypeStruct((m, D_MODEL), F32),
        grid=(m // DOWN_TM,),
        in_specs=[
            pl.BlockSpec((DOWN_TM, D_FF), lambda i: (i, 0)),
            const((D_FF, D_MODEL)),
            pl.BlockSpec((DOWN_TM, D_MODEL), lambda i: (i, 0)),
            const((1, D_MODEL)),
            const((1, D_MODEL)),
        ],
        out_specs=pl.BlockSpec((DOWN_TM, D_MODEL), lambda i: (i, 0)),
        compiler_params=_cparams(1, 60),
        name="ffn_down",
    )(g2d, w_down, x1, ln_g, ln_b)


def _z_col_scale():
    s = np.ones((1, IN_COLS), np.float32)
    s[0, :DIFF_WIDTH] = DIFF_QK_DIM ** -0.5 * LOG2E
    s[0, 3 * DIFF_WIDTH:3 * DIFF_WIDTH + NA_WIDTH] = NA_HEAD_DIM ** -0.5 * LOG2E
    return s


def _alibi_slopes():
    n = N_DIFF_HEADS
    return np.array([2.0 ** (-8.0 * (i + 1) / n) for i in range(n)], dtype=np.float32)


def kernel(x, w_in, b_in, lam_q1, lam_k1, lam_q2, lam_k2, subln_g, rpb, w_branch_a, w_branch_b, w_out,
           ln1_g, ln1_b, w_up, conv_w, conv_b, w_down, ln2_g, ln2_b):
    bsz, seq, d = x.shape
    assert (seq, d) == (SEQ, D_MODEL) and w_in.shape[0] == 1
    row = lambda a: a.reshape(1, -1).astype(F32)
    x2d = x.reshape(bsz * seq, d)

    z = _in_proj(x2d, w_in[0].astype(BF16), row(b_in[0]), jnp.asarray(_z_col_scale()))
    z3 = z.reshape(bsz, seq, IN_COLS)
    oa = _diff_attn(z3, jnp.asarray(_alibi_slopes()), row(lam_q1[0]), row(lam_k1[0]), row(lam_q2[0]),
                    row(lam_k2[0]), row(subln_g[0]))
    ob = _na_attn(z3, _na_bias_tables(rpb[0]))
    x1, x1b = _merge(oa.reshape(bsz * seq, DIFF_WIDTH), ob.reshape(bsz * seq, NA_WIDTH), z, x2d,
                     w_branch_a[0].astype(BF16), w_branch_b[0].astype(BF16), w_out[0].astype(BF16),
                     row(ln1_g[0]), row(ln1_b[0]))
    g = _ffn_up(x1b, w_up[0].astype(BF16), conv_w[0].astype(F32), row(conv_b[0]))
    out = _ffn_down(g, w_down[0].astype(BF16), x1, row(ln2_g[0]), row(ln2_b[0]))
    return out.reshape(bsz, seq, d)
```
